```python
import math
import jax, jax.numpy as jnp
from jax import lax
import numpy as np

D_MODEL = 1024
BATCH = 8
SEQ = 2048
DEPTH = 2
DEC_BATCH = 128
DEC_SEQ = 8
PAST_LEN = 16384
PAGE_SIZE = 128

N_PM = (DEPTH + 1) // 2
N_CV = DEPTH // 2
MIX_W = D_MODEL
POOL_W = MIX_W // 2
POOL_WINDOWS = (2, 4, 8, 16)
POOL_GROUPS = len(POOL_WINDOWS)
POOL_GW = POOL_W // POOL_GROUPS
POOL_STATE = max(POOL_WINDOWS) - 1
MLSTM_HEADS = 4
MLSTM_W = MIX_W - POOL_W
MLSTM_DH = MLSTM_W // MLSTM_HEADS
MLSTM_CHUNK = 64
PM_IN = POOL_W + 4 * MLSTM_W + 2 * MLSTM_HEADS
CONV_W = D_MODEL
CONV_K = 31
CONV_STATE = CONV_K - 1
N_EXPERTS = 32
TOP_K = 4
D_FF = D_MODEL
SWIGLU_LIMIT = 7.0
SWIGLU_ALPHA = 1.702
LN_EPS = 1e-5
DN_ALPHA = (2 * DEPTH) ** 0.25
DN_BETA = (8 * DEPTH) ** -0.25

kernel_name = 'pool_mlstm_conformer_moe_deepnorm_step'


def layer_norm(x, g, b):
    xf = x.astype(jnp.float32)
    mu = jnp.mean(xf, -1, keepdims=True)
    var = jnp.mean(jnp.square(xf - mu), -1, keepdims=True)
    y = (xf - mu) * lax.rsqrt(var + LN_EPS) * g.astype(jnp.float32) + b.astype(jnp.float32)
    return y.astype(x.dtype)


def pool_mix(u, u_past, pos0, w_pool, scale):
    B, T, _ = u.shape
    ext = jnp.concatenate([u_past.astype(u.dtype), u], axis=1)
    cs = jnp.pad(jnp.cumsum(ext.astype(jnp.float32), axis=1), ((0, 0), (1, 0), (0, 0)))
    pos = pos0 + jnp.arange(T)
    outs = []
    for g, w in enumerate(POOL_WINDOWS):
        csg = cs[:, :, g * POOL_GW:(g + 1) * POOL_GW]
        wsum = csg[:, POOL_STATE + 1:POOL_STATE + 1 + T] - csg[:, POOL_STATE + 1 - w:POOL_STATE + 1 - w + T]
        cnt = jnp.minimum(w, pos + 1).astype(jnp.float32)
        outs.append(wsum / cnt[None, :, None])
    pooled = jnp.concatenate(outs, axis=-1) - u.astype(jnp.float32)
    y = jnp.einsum('btgc,gcd->btgd', pooled.reshape(B, T, POOL_GROUPS, POOL_GW), w_pool.astype(jnp.float32))
    y = y.reshape(B, T, POOL_W) * scale.astype(jnp.float32)
    return y, ext[:, -POOL_STATE:]


def mlstm_chunkwise(q, k, v, i_pre, f_pre, C0, n0, m0):
    B, T, H, DH = q.shape
    L = MLSTM_CHUNK if T % MLSTM_CHUNK == 0 else T
    NC = T // L

    def chunks(a):
        return jnp.moveaxis(a.reshape((B, NC, L) + a.shape[2:]), 1, 0)

    q = q * (DH ** -0.5)
    logf = jax.nn.log_sigmoid(f_pre)
    causal = jnp.tril(jnp.ones((L, L), dtype=bool))

    def step(carry, inp):
        C, n, m = carry
        qc, kc, vc, ic, lfc = inp
        b = jnp.cumsum(lfc, axis=1).transpose(0, 2, 1)
        ih = ic.transpose(0, 2, 1)
        dmat = jnp.where(causal, b[:, :, :, None] - b[:, :, None, :] + ih[:, :, None, :], -jnp.inf)
        inter = b + m[:, :, None]
        m_t = jnp.maximum(inter, jnp.max(dmat, axis=-1))
        s = jnp.einsum('blhd,bshd->bhls', qc, kc) * jnp.exp(dmat - m_t[..., None])
        w_inter = jnp.exp(inter - m_t)
        num = jnp.einsum('bhls,bshd->blhd', s, vc) + jnp.einsum('bhl,blhk,bhkv->blhv', w_inter, qc, C)
        den = jnp.sum(s, -1) + w_inter * jnp.einsum('blhk,bhk->bhl', qc, n)
        bound = jnp.maximum(jnp.abs(den), jnp.exp(-m_t))
        h = num / bound.transpose(0, 2, 1)[..., None]
        b_end = b[:, :, -1]
        m_new = jnp.maximum(b_end + m, jnp.max(b_end[:, :, None] - b + ih, axis=-1))
        w_s = jnp.exp(b_end[:, :, None] - b + ih - m_new[:, :, None])
        decay = jnp.exp(b_end + m - m_new)
        C_new = decay[..., None, None] * C + jnp.einsum('bhs,bshk,bshv->bhkv', w_s, kc, vc)
        n_new = decay[..., None] * n + jnp.einsum('bhs,bshk->bhk', w_s, kc)
        return (C_new, n_new, m_new), h

    (C, n, m), hs = lax.scan(step, (C0, n0, m0),
                             (chunks(q), chunks(k), chunks(v), chunks(i_pre), chunks(logf)))
    h = jnp.moveaxis(hs, 0, 1).reshape(B, T, H, DH)
    return h, C, n, m


def pool_mlstm_mixer(x, pos0, st_pool, st_C, st_n, st_m, w_in, b_if, w_pool, pool_scale, hn_g, w_out, b_out):
    B, T, _ = x.shape
    f32 = jnp.float32
    z = x @ w_in
    u = z[..., :POOL_W]
    q, k, v, o = jnp.split(z[..., POOL_W:POOL_W + 4 * MLSTM_W], 4, axis=-1)
    gates = (z[..., POOL_W + 4 * MLSTM_W:] + b_if).astype(f32)
    i_pre, f_pre = gates[..., :MLSTM_HEADS], gates[..., MLSTM_HEADS:]
    y_pool, new_pool = pool_mix(u, st_pool, pos0, w_pool, pool_scale)
    hd = lambda a: a.reshape(B, T, MLSTM_HEADS, MLSTM_DH).astype(f32)
    h, C, n, m = mlstm_chunkwise(hd(q), hd(k), hd(v), i_pre, f_pre,
                                 st_C.astype(f32), st_n.astype(f32), st_m.astype(f32))
    mu = jnp.mean(h, -1, keepdims=True)
    var = jnp.mean(jnp.square(h - mu), -1, keepdims=True)
    hn = ((h - mu) * lax.rsqrt(var + LN_EPS)).reshape(B, T, MLSTM_W) * hn_g.astype(f32)
    y_m = jax.nn.sigmoid(o.astype(f32)) * hn
    y = jnp.concatenate([y_pool, y_m], axis=-1).astype(x.dtype) @ w_out + b_out
    return y, new_pool, C.astype(st_C.dtype), n.astype(st_n.dtype), m.astype(st_m.dtype)


def conv_mixer(x, st_conv, w_in, b_in, dw_w, dw_b, ln_g, ln_b, w_out, b_out):
    z = x @ w_in + b_in
    g = z[..., :CONV_W] * jax.nn.sigmoid(z[..., CONV_W:])
    ext = jnp.concatenate([st_conv.astype(g.dtype), g], axis=1)
    c = lax.conv_general_dilated(ext, dw_w[:, None, :].astype(ext.dtype), (1,), 'VALID',
                                 dimension_numbers=('NWC', 'WIO', 'NWC'),
                                 feature_group_count=CONV_W) + dw_b
    c = layer_norm(c, ln_g, ln_b)
    c = c * jax.nn.sigmoid(c)
    y = c @ w_out + b_out
    return y, ext[:, -CONV_STATE:]


def moe_ffn(x, w_r, b_r, w_gu, b_gu, w_dn, b_dn):
    B, T, D = x.shape
    xt = x.reshape(B * T, D)
    logits = (xt @ w_r + b_r).astype(jnp.float32)
    top_v, top_i = lax.top_k(logits, TOP_K)
    gates = jax.nn.softmax(top_v, axis=-1)
    gate_dense = jnp.einsum('nk,nke->ne', gates, jax.nn.one_hot(top_i, N_EXPERTS, dtype=jnp.float32))
    out = jnp.zeros((B * T, D), jnp.float32)
    for e in range(N_EXPERTS):
        gu = xt @ w_gu[e] + b_gu[e]
        g = jnp.minimum(gu[:, :D_FF], SWIGLU_LIMIT)
        u = jnp.clip(gu[:, D_FF:], -SWIGLU_LIMIT, SWIGLU_LIMIT)
        h = (u + 1.0) * g * jax.nn.sigmoid(SWIGLU_ALPHA * g)
        out = out + gate_dense[:, e:e + 1] * (h @ w_dn[e] + b_dn[e]).astype(jnp.float32)
    return out.reshape(B, T, D).astype(x.dtype)


def run_trunk(x, pos0, st_pool, st_C, st_n, st_m, st_conv, p):
    pools, Cs, ns, ms, convs = [], [], [], [], []
    for l in range(DEPTH):
        j = l // 2
        if l % 2 == 0:
            mix, pst, C, n, m = pool_mlstm_mixer(
                x, pos0, st_pool[j], st_C[j], st_n[j], st_m[j],
                p['w_in_pm'][j], p['b_if'][j], p['w_pool'][j], p['pool_scale'][j], p['hn_g'][j],
                p['w_out_pm'][j], p['b_out_pm'][j])
            pools.append(pst); Cs.append(C); ns.append(n); ms.append(m)
        else:
            mix, cst = conv_mixer(
                x, st_conv[j], p['w_in_cv'][j], p['b_in_cv'][j], p['dw_w'][j], p['dw_b'][j],
                p['cln_g'][j], p['cln_b'][j], p['w_out_cv'][j], p['b_out_cv'][j])
            convs.append(cst)
        x = layer_norm(DN_ALPHA * x + mix, p['ln_mix_g'][l], p['ln_mix_b'][l])
        ff = moe_ffn(x, p['w_router'][l], p['b_router'][l], p['w_gu'][l], p['b_gu'][l], p['w_dn'][l], p['b_dn'][l])
        x = layer_norm(DN_ALPHA * x + ff, p['ln_ffn_g'][l], p['ln_ffn_b'][l])
    return x, jnp.stack(pools), jnp.stack(Cs), jnp.stack(ns), jnp.stack(ms), jnp.stack(convs)


def setup_inputs(seed: int = 0) -> dict:
    key = jax.random.key(seed)
    ks = iter(jax.random.split(key, 48))
    f32 = jnp.float32

    def nrm(shape, scale):
        return scale * jax.random.normal(next(ks), shape, f32)

    f_bias = jnp.linspace(3.0, 6.0, MLSTM_HEADS, dtype=f32)[None, :] + nrm((N_PM, MLSTM_HEADS), 0.1)
    i_bias = nrm((N_PM, MLSTM_HEADS), 0.1)
    return {
        'x_prompt': nrm((BATCH, SEQ, D_MODEL), 1.0),
        'x_sample': nrm((DEC_BATCH, DEC_SEQ, D_MODEL), 1.0),
        'state_pool': nrm((N_PM, DEC_BATCH, POOL_STATE, POOL_W), 1.0),
        'state_mlstm_C': nrm((N_PM, DEC_BATCH, MLSTM_HEADS, MLSTM_DH, MLSTM_DH), 0.3),
        'state_mlstm_n': nrm((N_PM, DEC_BATCH, MLSTM_HEADS, MLSTM_DH), 0.3),
        'state_mlstm_m': nrm((N_PM, DEC_BATCH, MLSTM_HEADS), 1.0),
        'state_conv': nrm((N_CV, DEC_BATCH, CONV_STATE, CONV_W), 0.5),
        'w_in_pm': nrm((N_PM, D_MODEL, PM_IN), D_MODEL ** -0.5),
        'b_if': jnp.concatenate([i_bias, f_bias], axis=-1),
        'w_pool': nrm((N_PM, POOL_GROUPS, POOL_GW, POOL_GW), POOL_GW ** -0.5),
        'pool_scale': 1.0 + nrm((N_PM, POOL_W), 0.02),
        'hn_g': 1.0 + nrm((N_PM, MLSTM_W), 0.02),
        'w_out_pm': nrm((N_PM, MIX_W, D_MODEL), DN_BETA * MIX_W ** -0.5),
        'b_out_pm': nrm((N_PM, D_MODEL), 0.02),
        'w_in_cv': nrm((N_CV, D_MODEL, 2 * CONV_W), D_MODEL ** -0.5),
        'b_in_cv': nrm((N_CV, 2 * CONV_W), 0.02),
        'dw_w': nrm((N_CV, CONV_K, CONV_W), CONV_K ** -0.5),
        'dw_b': nrm((N_CV, CONV_W), 0.02),
        'cln_g': 1.0 + nrm((N_CV, CONV_W), 0.02),
        'cln_b': nrm((N_CV, CONV_W), 0.02),
        'w_out_cv': nrm((N_CV, CONV_W, D_MODEL), DN_BETA * CONV_W ** -0.5),
        'b_out_cv': nrm((N_CV, D_MODEL), 0.02),
        'ln_mix_g': 1.0 + nrm((DEPTH, D_MODEL), 0.02),
        'ln_mix_b': nrm((DEPTH, D_MODEL), 0.02),
        'ln_ffn_g': 1.0 + nrm((DEPTH, D_MODEL), 0.02),
        'ln_ffn_b': nrm((DEPTH, D_MODEL), 0.02),
        'w_router': nrm((DEPTH, D_MODEL, N_EXPERTS), D_MODEL ** -0.5),
        'b_router': nrm((DEPTH, N_EXPERTS), 0.01),
        'w_gu': nrm((DEPTH, N_EXPERTS, D_MODEL, 2 * D_FF), D_MODEL ** -0.5),
        'b_gu': nrm((DEPTH, N_EXPERTS, 2 * D_FF), 0.02),
        'w_dn': nrm((DEPTH, N_EXPERTS, D_FF, D_MODEL), DN_BETA * D_FF ** -0.5),
        'b_dn': nrm((DEPTH, N_EXPERTS, D_MODEL), 0.02),
    }


def reference(x_prompt, x_sample, state_pool, state_mlstm_C, state_mlstm_n, state_mlstm_m, state_conv,
              w_in_pm, b_if, w_pool, pool_scale, hn_g, w_out_pm, b_out_pm,
              w_in_cv, b_in_cv, dw_w, dw_b, cln_g, cln_b, w_out_cv, b_out_cv,
              ln_mix_g, ln_mix_b, ln_ffn_g, ln_ffn_b,
              w_router, b_router, w_gu, b_gu, w_dn, b_dn):
    p = dict(w_in_pm=w_in_pm, b_if=b_if, w_pool=w_pool, pool_scale=pool_scale, hn_g=hn_g,
             w_out_pm=w_out_pm, b_out_pm=b_out_pm, w_in_cv=w_in_cv, b_in_cv=b_in_cv,
             dw_w=dw_w, dw_b=dw_b, cln_g=cln_g, cln_b=cln_b, w_out_cv=w_out_cv, b_out_cv=b_out_cv,
             ln_mix_g=ln_mix_g, ln_mix_b=ln_mix_b, ln_ffn_g=ln_ffn_g, ln_ffn_b=ln_ffn_b,
             w_router=w_router, b_router=b_router, w_gu=w_gu, b_gu=b_gu, w_dn=w_dn, b_dn=b_dn)
    B = x_prompt.shape[0]
    dt = x_prompt.dtype
    z_pool = jnp.zeros((N_PM, B, POOL_STATE, POOL_W), dt)
    z_C = jnp.zeros((N_PM, B, MLSTM_HEADS, MLSTM_DH, MLSTM_DH), dt)
    z_n = jnp.zeros((N_PM, B, MLSTM_HEADS, MLSTM_DH), dt)
    z_m = jnp.zeros((N_PM, B, MLSTM_HEADS), dt)
    z_conv = jnp.zeros((N_CV, B, CONV_STATE, CONV_W), dt)
    y_prompt, pool_p, C_p, n_p, m_p, conv_p = run_trunk(x_prompt, 0, z_pool, z_C, z_n, z_m, z_conv, p)
    y_sample, pool_s, C_s, n_s, m_s, conv_s = run_trunk(
        x_sample, PAST_LEN, state_pool, state_mlstm_C, state_mlstm_n, state_mlstm_m, state_conv, p)
    return (y_prompt, y_sample, pool_p, pool_s, C_p, C_s, n_p, n_s, m_p, m_s, conv_p, conv_s)
```

```python
import functools

import jax
import jax.numpy as jnp
from jax import lax
from jax.experimental import pallas as pl
from jax.experimental.pallas import tpu as pltpu

F32 = jnp.float32
BF16 = jnp.bfloat16
I32 = jnp.int32

D_MODEL = 1024
DEPTH = 2
POOL_W = 512
POOL_WINDOWS = (2, 4, 8, 16)
POOL_GW = 128
POOL_STATE = 15
POOL_HALO = 16
MLSTM_HEADS = 4
MLSTM_W = 512
MLSTM_DH = 128
MLSTM_CHUNK = 64
CONV_W = 1024
CONV_K = 31
CONV_STATE = 30
CONV_HALO = 32
N_EXPERTS = 32
TOP_K = 4
D_FF = 1024
SWIGLU_LIMIT = 7.0
SWIGLU_ALPHA = 1.702
LN_EPS = 1e-5
DN_ALPHA = (2 * DEPTH) ** 0.25
PAST_LEN = 16384

LANES = 128
TOKEN_TILE = 512
MOE_TILE = 512
DMA_TILE = 512
NEG_BIG = -1e30
VMEM_LIMIT = 56 * 1024 * 1024


def _cparams(sem, vmem=VMEM_LIMIT):
    return pltpu.CompilerParams(dimension_semantics=sem, vmem_limit_bytes=vmem)


def _layer_norm(x, g, b):
    mu = jnp.mean(x, -1, keepdims=True)
    xc = x - mu
    var = jnp.mean(xc * xc, -1, keepdims=True)
    return xc * lax.rsqrt(var + LN_EPS) * g + b


def _log_sigmoid(x):
    return jnp.minimum(x, 0.0) - jnp.log1p(jnp.exp(-jnp.abs(x)))


def _dot(a, b):
    return jnp.dot(a, b, preferred_element_type=F32)


def _pm_in_kernel(x_ref, w_ref, wg_ref, bg_ref, u_ref, q_ref, k_ref, v_ref, o_ref, g_ref):
    xb = x_ref[...].astype(BF16)
    for j, out in enumerate((u_ref, q_ref, k_ref, v_ref, o_ref)):
        out[...] = _dot(xb, w_ref[:, j * 512:(j + 1) * 512])
    g_ref[...] = _dot(xb, wg_ref[...]) + bg_ref[...]


def _pm_in_proj(x, w_main, w_gate, b_gate):
    n = x.shape[0]
    tm = min(TOKEN_TILE, n)
    row = lambda c: pl.BlockSpec((tm, c), lambda i: (i, 0))
    full = lambda a: pl.BlockSpec(a.shape, lambda i: (0,) * a.ndim)
    return pl.pallas_call(
        _pm_in_kernel,
        grid=(n // tm,),
        in_specs=[row(D_MODEL), full(w_main), full(w_gate), full(b_gate)],
        out_specs=[row(512)] * 5 + [row(LANES)],
        out_shape=[jax.ShapeDtypeStruct((n, 512), F32)] * 5 + [jax.ShapeDtypeStruct((n, LANES), F32)],
        compiler_params=_cparams(("parallel",)),
        name="pm_in_proj",
    )(x, w_main, w_gate, b_gate)


def _pool_kernel(u_ref, past_ref, w_ref, sc_ref, y_ref, ext, *, tt, pos0):
    j = pl.program_id(1)

    @pl.when(j == 0)
    def _():
        ext[:, 0:POOL_HALO, :] = past_ref[...]

    u = u_ref[...]
    ext[:, POOL_HALO:POOL_HALO + tt, :] = u
    bb = u.shape[0]
    pos = pos0 + j * tt + lax.broadcasted_iota(I32, (tt, 1), 0)
    for g, w in enumerate(POOL_WINDOWS):
        lanes = slice(g * POOL_GW, (g + 1) * POOL_GW)
        wsum = ext[:, POOL_HALO:POOL_HALO + tt, lanes]
        for d in range(1, w):
            wsum = wsum + ext[:, POOL_HALO - d:POOL_HALO - d + tt, lanes]
        cnt = jnp.minimum(w, pos + 1).astype(F32)
        pooled = wsum / cnt[None] - u[:, :, lanes]
        y = _dot(pooled.reshape(bb * tt, POOL_GW).astype(BF16), w_ref[g])
        y_ref[:, :, lanes] = y.reshape(bb, tt, POOL_GW) * sc_ref[:, lanes][None]
    ext[:, 0:POOL_HALO, :] = ext[:, tt:tt + POOL_HALO, :]


def _pool_mix(u3, past, w_pool, scale, *, bb, tt, pos0):
    b, t, _ = u3.shape
    return pl.pallas_call(
        functools.partial(_pool_kernel, tt=tt, pos0=pos0),
        grid=(b // bb, t // tt),
        in_specs=[
            pl.BlockSpec((bb, tt, POOL_W), lambda i, j: (i, j, 0)),
            pl.BlockSpec((bb, POOL_HALO, POOL_W), lambda i, j: (i, 0, 0)),
            pl.BlockSpec(w_pool.shape, lambda i, j: (0, 0, 0)),
            pl.BlockSpec(scale.shape, lambda i, j: (0, 0)),
        ],
        out_specs=pl.BlockSpec((bb, tt, POOL_W), lambda i, j: (i, j, 0)),
        out_shape=jax.ShapeDtypeStruct((b, t, POOL_W), F32),
        scratch_shapes=[pltpu.VMEM((bb, POOL_HALO + tt, POOL_W), F32)],
        compiler_params=_cparams(("parallel", "arbitrary")),
        name="pool_mix",
    )(u3, past, w_pool, scale)


def _mlstm_kernel(q_ref, k_ref, v_ref, o_ref, g_ref, c0_ref, n0_ref, m0_ref, hng_ref,
                  y_ref, c_out, n_out, m_out, cs, ns, ms, *, bb, l):
    c = pl.program_id(1)

    @pl.when(c == 0)
    def _():
        cs[...] = c0_ref[...]
        ns[...] = n0_ref[...]
        ms[...] = m0_ref[...]

    row = lax.broadcasted_iota(I32, (l, l), 0)
    col = lax.broadcasted_iota(I32, (l, l), 1)
    causal = row >= col
    tril = causal.astype(F32)
    lane = lax.broadcasted_iota(I32, (l, LANES), 1)
    sel = (lax.broadcasted_iota(I32, (8, LANES), 0) == lax.broadcasted_iota(I32, (8, LANES), 1)).astype(F32)
    scale = MLSTM_DH ** -0.5
    padr = (-l) % LANES if l < LANES and l % 16 else 0

    def per_seq(b, carry):
        gates = g_ref[b]
        logf = _log_sigmoid(gates)
        bcum = jnp.dot(tril, logf, precision=lax.Precision.HIGHEST, preferred_element_type=F32)
        cols = jnp.where(lane < MLSTM_HEADS, gates, bcum)
        rows = lax.dot_general(sel, cols, (((1,), (1,)), ((), ())),
                               precision=lax.Precision.HIGHEST, preferred_element_type=F32)
        for h in range(MLSTM_HEADS):
            hl = slice(h * MLSTM_DH, (h + 1) * MLSTM_DH)
            i_col = cols[:, h:h + 1]
            b_col = cols[:, MLSTM_HEADS + h:MLSTM_HEADS + h + 1]
            i_row = rows[h:h + 1, :]
            b_row = rows[MLSTM_HEADS + h:MLSTM_HEADS + h + 1, :]
            m_prev = ms[b, h][:, 0:1]
            c_prev = cs[b, h]
            n_prev = ns[b, h]
            qh = q_ref[b, :, hl] * scale
            kh = k_ref[b, :, hl]
            vh = v_ref[b, :, hl]

            dmat = jnp.where(causal, b_col - b_row + i_row, -jnp.inf)
            inter = b_col + m_prev
            m_t = jnp.maximum(inter, jnp.max(dmat, axis=-1, keepdims=True))
            qk = lax.dot_general(qh, kh, (((1,), (1,)), ((), ())), preferred_element_type=F32)
            s = qk * jnp.exp(dmat - m_t)
            w_inter = jnp.exp(inter - m_t)
            num = _dot(s, vh) + w_inter * _dot(qh, c_prev)
            den = jnp.sum(s, -1, keepdims=True) + w_inter * jnp.sum(qh * n_prev, -1, keepdims=True)
            bound = jnp.maximum(jnp.abs(den), jnp.exp(-m_t))
            hh = num / bound

            b_end = b_col[l - 1:l, :]
            gcol = b_end - b_col + i_col
            m_new = jnp.maximum(b_end + m_prev, jnp.max(gcol, axis=0, keepdims=True))
            kw = kh * jnp.exp(gcol - m_new)
            decay = jnp.exp(b_end + m_prev - m_new)
            if padr:
                zpad = jnp.zeros((padr, MLSTM_DH), F32)
                kv = lax.dot_general(jnp.concatenate([kw, zpad], 0), jnp.concatenate([vh, zpad], 0),
                                     (((0,), (0,)), ((), ())), preferred_element_type=F32)
            else:
                kv = lax.dot_general(kw, vh, (((0,), (0,)), ((), ())), preferred_element_type=F32)
            cs[b, h] = decay * c_prev + kv
            ns[b, h] = decay * n_prev + jnp.sum(kw, axis=0, keepdims=True)
            ms[b, h] = jnp.broadcast_to(m_new, (1, LANES))

            mu = jnp.mean(hh, -1, keepdims=True)
            hc = hh - mu
            var = jnp.mean(hc * hc, -1, keepdims=True)
            hn = hc * lax.rsqrt(var + LN_EPS) * hng_ref[:, hl]
            y_ref[b, :, hl] = jax.nn.sigmoid(o_ref[b, :, hl]) * hn
        return carry

    lax.fori_loop(0, bb, per_seq, 0)

    @pl.when(c == pl.num_programs(1) - 1)
    def _():
        c_out[...] = cs[...]
        n_out[...] = ns[...]
        m_out[...] = ms[...]


def _mlstm(q3, k3, v3, o3, g3, c0, n0, m0, hn_g, *, bb, l):
    b, t, _ = q3.shape
    seq = lambda c: pl.BlockSpec((bb, l, c), lambda i, j: (i, j, 0))
    st_c = pl.BlockSpec((bb, MLSTM_HEADS, MLSTM_DH, MLSTM_DH), lambda i, j: (i, 0, 0, 0))
    st_v = pl.BlockSpec((bb, MLSTM_HEADS, 1, LANES), lambda i, j: (i, 0, 0, 0))
    return pl.pallas_call(
        functools.partial(_mlstm_kernel, bb=bb, l=l),
        grid=(b // bb, t // l),
        in_specs=[seq(MLSTM_W)] * 4 + [seq(LANES), st_c, st_v, st_v,
                                       pl.BlockSpec(hn_g.shape, lambda i, j: (0, 0))],
        out_specs=[seq(MLSTM_W), st_c, st_v, st_v],
        out_shape=[
            jax.ShapeDtypeStruct((b, t, MLSTM_W), F32),
            jax.ShapeDtypeStruct((b, MLSTM_HEADS, MLSTM_DH, MLSTM_DH), F32),
            jax.ShapeDtypeStruct((b, MLSTM_HEADS, 1, LANES), F32),
            jax.ShapeDtypeStruct((b, MLSTM_HEADS, 1, LANES), F32),
        ],
        scratch_shapes=[
            pltpu.VMEM((bb, MLSTM_HEADS, MLSTM_DH, MLSTM_DH), F32),
            pltpu.VMEM((bb, MLSTM_HEADS, 1, LANES), F32),
            pltpu.VMEM((bb, MLSTM_HEADS, 1, LANES), F32),
        ],
        compiler_params=_cparams(("parallel", "arbitrary")),
        name="mlstm",
    )(q3, k3, v3, o3, g3, c0, n0, m0, hn_g)


def _mix_out_kernel(*refs, n_parts):
    parts = refs[:n_parts]
    ws = refs[n_parts:2 * n_parts]
    (x_ref, bo_ref, g_ref, b_ref, wr_ref, br_ref, cin_ref,
     x1_ref, idx_ref, rank_ref, gate_ref, cnt_ref, carry) = refs[2 * n_parts:]
    i = pl.program_id(0)

    @pl.when(i == 0)
    def _():
        carry[...] = cin_ref[...]

    mix = bo_ref[...]
    for a_ref, w_ref in zip(parts, ws):
        mix = mix + _dot(a_ref[...].astype(BF16), w_ref[...])
    x1 = _layer_norm(DN_ALPHA * x_ref[...] + mix, g_ref[...], b_ref[...])
    x1_ref[...] = x1

    tm = x1.shape[0]
    logits = _dot(x1.astype(BF16), wr_ref[...]) + br_ref[...]
    lane = lax.broadcasted_iota(I32, (tm, LANES), 1)
    work = logits
    picks, vals = [], []
    onehot = jnp.zeros((tm, LANES), F32)
    for _ in range(TOP_K):
        mx = jnp.max(work, axis=-1, keepdims=True)
        pick = jnp.min(jnp.where(work == mx, lane, LANES), axis=-1, keepdims=True)
        hit = lane == pick
        onehot = jnp.where(hit, 1.0, onehot)
        work = jnp.where(hit, -jnp.inf, work)
        picks.append(pick)
        vals.append(mx)
    exps = [jnp.exp(v - vals[0]) for v in vals]
    tot = exps[0] + exps[1] + exps[2] + exps[3]

    rr = lax.broadcasted_iota(I32, (tm, tm), 0)
    cc = lax.broadcasted_iota(I32, (tm, tm), 1)
    before = _dot((rr > cc).astype(BF16), onehot.astype(BF16)) + carry[...]
    idx_o = jnp.zeros((tm, LANES), I32)
    rank_o = jnp.zeros((tm, LANES), I32)
    gate_o = jnp.zeros((tm, LANES), F32)
    for kk in range(TOP_K):
        rk = jnp.sum(jnp.where(lane == picks[kk], before, 0.0), axis=-1, keepdims=True)
        idx_o = jnp.where(lane == kk, picks[kk], idx_o)
        rank_o = jnp.where(lane == kk, rk.astype(I32), rank_o)
        gate_o = jnp.where(lane == kk, exps[kk] / tot, gate_o)
    idx_ref[...] = idx_o
    rank_ref[...] = rank_o
    gate_ref[...] = gate_o
    carry[...] = carry[...] + jnp.sum(onehot, axis=0, keepdims=True)
    cnt_ref[...] = carry[...]


def _mix_out(parts, ws, x, b_out, ln_g, ln_b, w_r, b_r, carry_in):
    n = x.shape[0]
    tm = min(TOKEN_TILE, n)
    row = lambda c: pl.BlockSpec((tm, c), lambda i: (i, 0))
    full = lambda a: pl.BlockSpec(a.shape, lambda i: (0,) * a.ndim)
    small = [b_out, ln_g, ln_b, w_r, b_r, carry_in]
    return pl.pallas_call(
        functools.partial(_mix_out_kernel, n_parts=len(parts)),
        grid=(n // tm,),
        in_specs=[row(p.shape[1]) for p in parts] + [full(w) for w in ws] + [row(D_MODEL)] + [full(a) for a in small],
        out_specs=[row(D_MODEL), row(LANES), row(LANES), row(LANES), pl.BlockSpec((1, LANES), lambda i: (0, 0))],
        out_shape=[
            jax.ShapeDtypeStruct((n, D_MODEL), F32),
            jax.ShapeDtypeStruct((n, LANES), I32),
            jax.ShapeDtypeStruct((n, LANES), I32),
            jax.ShapeDtypeStruct((n, LANES), F32),
            jax.ShapeDtypeStruct((1, LANES), F32),
        ],
        scratch_shapes=[pltpu.VMEM((1, LANES), F32)],
        compiler_params=_cparams(("arbitrary",)),
        name="mix_out_router",
    )(*parts, *ws, x, *small)


def _row_copy(src, s, dst, d, sem):
    return pltpu.make_async_copy(src.at[pl.ds(s, 1), :], dst.at[pl.ds(d, 1), :], sem)


def _dispatch_kernel(slot_p, slot_s, xp_ref, xs_ref, out_ref, smem, sem_i, sem_d, *, tiles_p, td):
    i = pl.program_id(0)

    def run(slot_hbm, x_ref, tile):
        cp = pltpu.make_async_copy(slot_hbm.at[pl.ds(tile * (td * TOP_K), td * TOP_K)], smem, sem_i)
        cp.start()
        cp.wait()

        def issue(t, carry):
            for kk in range(TOP_K):
                _row_copy(x_ref, tile * td + t, out_ref, smem[t * TOP_K + kk], sem_d).start()
            return carry

        lax.fori_loop(0, td, issue, 0)

        def drain(t, carry):
            for kk in range(TOP_K):
                _row_copy(x_ref, tile * td + t, out_ref, smem[t * TOP_K + kk], sem_d).wait()
            return carry

        lax.fori_loop(0, td, drain, 0)

    @pl.when(i < tiles_p)
    def _():
        run(slot_p, xp_ref, i)

    @pl.when(i >= tiles_p)
    def _():
        run(slot_s, xs_ref, i - tiles_p)


def _dispatch(slot_p, slot_s, x_p, x_s):
    td = DMA_TILE
    tiles_p, tiles_s = x_p.shape[0] // td, x_s.shape[0] // td
    n_slots = (x_p.shape[0] + x_s.shape[0]) * TOP_K
    any_spec = pl.BlockSpec(memory_space=pl.ANY)
    return pl.pallas_call(
        functools.partial(_dispatch_kernel, tiles_p=tiles_p, td=td),
        grid=(tiles_p + tiles_s,),
        in_specs=[any_spec] * 4,
        out_specs=any_spec,
        out_shape=jax.ShapeDtypeStruct((n_slots, D_MODEL), F32),
        scratch_shapes=[pltpu.SMEM((td * TOP_K,), I32), pltpu.SemaphoreType.DMA, pltpu.SemaphoreType.DMA],
        compiler_params=_cparams(("arbitrary",)),
        name="moe_dispatch",
    )(slot_p, slot_s, x_p, x_s)


def _combine_kernel(slot_hbm, ys_ref, out_ref, smem, sem_i, sem_d, *, td, n):
    i = pl.program_id(0)
    cp = pltpu.make_async_copy(slot_hbm.at[pl.ds(i * (td * TOP_K), td * TOP_K)], smem, sem_i)
    cp.start()
    cp.wait()

    def issue(t, carry):
        for kk in range(TOP_K):
            _row_copy(ys_ref, smem[t * TOP_K + kk], out_ref, kk * n + i * td + t, sem_d).start()
        return carry

    lax.fori_loop(0, td, issue, 0)

    def drain(t, carry):
        for kk in range(TOP_K):
            _row_copy(ys_ref, smem[t * TOP_K + kk], out_ref, kk * n + i * td + t, sem_d).wait()
        return carry

    lax.fori_loop(0, td, drain, 0)


def _combine(slot, ys, n):
    td = min(DMA_TILE, n)
    any_spec = pl.BlockSpec(memory_space=pl.ANY)
    return pl.pallas_call(
        functools.partial(_combine_kernel, td=td, n=n),
        grid=(n // td,),
        in_specs=[any_spec, any_spec],
        out_specs=any_spec,
        out_shape=jax.ShapeDtypeStruct((TOP_K * n, D_MODEL), F32),
        scratch_shapes=[pltpu.SMEM((td * TOP_K,), I32), pltpu.SemaphoreType.DMA, pltpu.SemaphoreType.DMA],
        compiler_params=_cparams(("arbitrary",)),
        name="moe_combine",
    )(slot, ys)


def _moe_kernel(w_tile, w_exp, w_lo, w_hi, w_first, w_newexp, w_valid,
                xs_ref, wgu_ref, bgu_ref, wdn_ref, bdn_ref, ys_ref, wgu_b, wdn_b):
    j = pl.program_id(0)

    @pl.when(w_newexp[j] == 1)
    def _():
        wgu_b[...] = wgu_ref[0].astype(BF16)
        wdn_b[...] = wdn_ref[0].astype(BF16)

    @pl.when(w_first[j] == 1)
    def _():
        ys_ref[...] = jnp.zeros_like(ys_ref)

    @pl.when(w_valid[j] == 1)
    def _():
        tm = xs_ref.shape[0]
        gu = _dot(xs_ref[...].astype(BF16), wgu_b[...]) + bgu_ref[0]
        g = jnp.minimum(gu[:, :D_FF], SWIGLU_LIMIT)
        u = jnp.clip(gu[:, D_FF:], -SWIGLU_LIMIT, SWIGLU_LIMIT)
        h = (u + 1.0) * g * jax.nn.sigmoid(SWIGLU_ALPHA * g)
        y = _dot(h.astype(BF16), wdn_b[...]) + bdn_ref[0]
        r = lax.broadcasted_iota(I32, (tm, 1), 0)
        mine = (r >= w_lo[j]) & (r < w_hi[j])
        ys_ref[...] = jnp.where(mine, y, ys_ref[...])


def _moe_experts(work, xs, w_gu, b_gu, w_dn, b_dn):
    n_slots = xs.shape[0]
    n_work = work[0].shape[0]
    tm = MOE_TILE
    grid_spec = pltpu.PrefetchScalarGridSpec(
        num_scalar_prefetch=7,
        grid=(n_work,),
        in_specs=[
            pl.BlockSpec((tm, D_MODEL), lambda j, wt, we, *_: (wt[j], 0)),
            pl.BlockSpec((1, D_MODEL, 2 * D_FF), lambda j, wt, we, *_: (we[j], 0, 0)),
            pl.BlockSpec((1, 1, 2 * D_FF), lambda j, wt, we, *_: (we[j], 0, 0)),
            pl.BlockSpec((1, D_FF, D_MODEL), lambda j, wt, we, *_: (we[j], 0, 0)),
            pl.BlockSpec((1, 1, D_MODEL), lambda j, wt, we, *_: (we[j], 0, 0)),
        ],
        out_specs=pl.BlockSpec((tm, D_MODEL), lambda j, wt, we, *_: (wt[j], 0)),
        scratch_shapes=[pltpu.VMEM((D_MODEL, 2 * D_FF), BF16), pltpu.VMEM((D_FF, D_MODEL), BF16)],
    )
    return pl.pallas_call(
        _moe_kernel,
        grid_spec=grid_spec,
        out_shape=jax.ShapeDtypeStruct((n_slots, D_MODEL), F32),
        compiler_params=_cparams(("arbitrary",)),
        name="moe_experts",
    )(*work, xs, w_gu, b_gu, w_dn, b_dn)


def _moe_work_items(counts, n_slots):
    tm = MOE_TILE
    n_tiles = n_slots // tm
    ends = jnp.cumsum(counts)
    offs = ends - counts
    starts = jnp.sort(jnp.concatenate([jnp.arange(n_tiles, dtype=I32) * tm, offs]))
    stops = jnp.concatenate([starts[1:], jnp.array([n_slots], I32)])
    valid = stops > starts
    order = jnp.argsort(jnp.logical_not(valid), stable=True)
    n_valid = jnp.sum(valid.astype(I32))
    pos = jnp.minimum(jnp.arange(starts.shape[0], dtype=I32), n_valid - 1)
    starts, stops = starts[order][pos], stops[order][pos]
    is_valid = (jnp.arange(starts.shape[0], dtype=I32) < n_valid).astype(I32)
    tile = starts // tm
    expert = jnp.minimum(jnp.searchsorted(ends, starts, side="right").astype(I32), N_EXPERTS - 1)
    prev = lambda a: jnp.concatenate([jnp.array([-1], I32), a[:-1]])
    first = (tile != prev(tile)).astype(I32) * is_valid
    newexp = (expert != prev(expert)).astype(I32) * is_valid
    lo = starts - tile * tm
    hi = stops - tile * tm
    return offs, (tile, expert, lo, hi, first, newexp, is_valid)


def _ffn_ln_kernel(x_ref, yk_ref, gate_ref, g_ref, b_ref, out_ref):
    gate = gate_ref[...]
    ff = gate[:, 0:1] * yk_ref[0]
    for kk in range(1, TOP_K):
        ff = ff + gate[:, kk:kk + 1] * yk_ref[kk]
    out_ref[...] = _layer_norm(DN_ALPHA * x_ref[...] + ff, g_ref[...], b_ref[...])


def _ffn_ln(x, yk, gate, ln_g, ln_b):
    n = x.shape[0]
    tm = min(TOKEN_TILE, n)
    row = lambda c: pl.BlockSpec((tm, c), lambda i: (i, 0))
    vec = pl.BlockSpec((1, D_MODEL), lambda i: (0, 0))
    return pl.pallas_call(
        _ffn_ln_kernel,
        grid=(n // tm,),
        in_specs=[row(D_MODEL), pl.BlockSpec((TOP_K, tm, D_MODEL), lambda i: (0, i, 0)), row(LANES), vec, vec],
        out_specs=row(D_MODEL),
        out_shape=jax.ShapeDtypeStruct((n, D_MODEL), F32),
        compiler_params=_cparams(("parallel",)),
        name="ffn_ln",
    )(x, yk, gate, ln_g, ln_b)


def _cv_in_kernel(x_ref, w_ref, b_ref, g_ref):
    xb = x_ref[...].astype(BF16)
    a = _dot(xb, w_ref[:, :CONV_W]) + b_ref[:, :CONV_W]
    gate = _dot(xb, w_ref[:, CONV_W:]) + b_ref[:, CONV_W:]
    g_ref[...] = a * jax.nn.sigmoid(gate)


def _cv_in_proj(x, w, b):
    n = x.shape[0]
    tm = min(TOKEN_TILE, n)
    row = lambda c: pl.BlockSpec((tm, c), lambda i: (i, 0))
    full = lambda a: pl.BlockSpec(a.shape, lambda i: (0,) * a.ndim)
    return pl.pallas_call(
        _cv_in_kernel,
        grid=(n // tm,),
        in_specs=[row(D_MODEL), full(w), full(b)],
        out_specs=row(CONV_W),
        out_shape=jax.ShapeDtypeStruct((n, CONV_W), F32),
        compiler_params=_cparams(("parallel",)),
        name="cv_in_proj",
    )(x, w, b)


def _dwconv_kernel(g_ref, past_ref, w_ref, wb_ref, lg_ref, lb_ref, c_ref, ext, *, tt):
    j = pl.program_id(1)

    @pl.when(j == 0)
    def _():
        ext[:, 0:CONV_HALO, :] = past_ref[...]

    ext[:, CONV_HALO:CONV_HALO + tt, :] = g_ref[...]
    base = CONV_HALO - CONV_STATE

    def per_lane_tile(ci, carry):
        lanes = pl.ds(pl.multiple_of(ci * LANES, LANES), LANES)
        acc = ext[:, base:base + tt, lanes] * w_ref[0:1, lanes][None]
        for d in range(1, CONV_K):
            acc = acc + ext[:, base + d:base + d + tt, lanes] * w_ref[d:d + 1, lanes][None]
        c_ref[:, :, lanes] = acc + wb_ref[:, lanes][None]
        return carry

    lax.fori_loop(0, CONV_W // LANES, per_lane_tile, 0)
    c = _layer_norm(c_ref[...], lg_ref[...][None], lb_ref[...][None])
    c_ref[...] = c * jax.nn.sigmoid(c)
    ext[:, 0:CONV_HALO, :] = ext[:, tt:tt + CONV_HALO, :]


def _dwconv(g3, past, dw_w, dw_b, ln_g, ln_b, *, bb, tt):
    b, t, _ = g3.shape
    vec = pl.BlockSpec((1, CONV_W), lambda i, j: (0, 0))
    return pl.pallas_call(
        functools.partial(_dwconv_kernel, tt=tt),
        grid=(b // bb, t // tt),
        in_specs=[
            pl.BlockSpec((bb, tt, CONV_W), lambda i, j: (i, j, 0)),
            pl.BlockSpec((bb, CONV_HALO, CONV_W), lambda i, j: (i, 0, 0)),
            pl.BlockSpec(dw_w.shape, lambda i, j: (0, 0)),
            vec, vec, vec,
        ],
        out_specs=pl.BlockSpec((bb, tt, CONV_W), lambda i, j: (i, j, 0)),
        out_shape=jax.ShapeDtypeStruct((b, t, CONV_W), F32),
        scratch_shapes=[pltpu.VMEM((bb, CONV_HALO + tt, CONV_W), F32)],
        compiler_params=_cparams(("parallel", "arbitrary")),
        name="dwconv_ln_swish",
    )(g3, past, dw_w, dw_b, ln_g, ln_b)


def _moe_block(l, x1, route, ln_g, ln_b, w_gu, b_gu, w_dn, b_dn):
    (idx_p, rank_p, gate_p), (idx_s, rank_s, gate_s), counts = route
    n_p, n_s = x1[0].shape[0], x1[1].shape[0]
    n_slots = (n_p + n_s) * TOP_K
    offs, work = _moe_work_items(counts, n_slots)
    slot_p = (rank_p[:, :TOP_K] + offs[idx_p[:, :TOP_K]]).reshape(-1)
    slot_s = (rank_s[:, :TOP_K] + offs[idx_s[:, :TOP_K]]).reshape(-1)
    xs = _dispatch(slot_p, slot_s, x1[0], x1[1])
    ys = _moe_experts(work, xs, w_gu[l], b_gu[l][:, None, :], w_dn[l], b_dn[l][:, None, :])
    out = []
    for x, slot, gate in ((x1[0], slot_p, gate_p), (x1[1], slot_s, gate_s)):
        n = x.shape[0]
        yk = _combine(slot, ys, n).reshape(TOP_K, n, D_MODEL)
        out.append(_ffn_ln(x, yk, gate, ln_g[l][None], ln_b[l][None]))
    return out


def _route_args(w_router, b_router, l):
    w_r = jnp.pad(w_router[l], ((0, 0), (0, LANES - N_EXPERTS))).astype(BF16)
    b_r = jnp.pad(b_router[l], (0, LANES - N_EXPERTS), constant_values=NEG_BIG)[None]
    return w_r, b_r


def kernel(x_prompt, x_sample, state_pool, state_mlstm_C, state_mlstm_n, state_mlstm_m, state_conv, w_in_pm, b_if, w_pool, pool_scale, hn_g, w_out_pm, b_out_pm, w_in_cv, b_in_cv, dw_w, dw_b, cln_g, cln_b, w_out_cv, b_out_cv, ln_mix_g, ln_mix_b, ln_ffn_g, ln_ffn_b, w_router, b_router, w_gu, b_gu, w_dn, b_dn):
    bp, tp, _ = x_prompt.shape
    bs, ts, _ = x_sample.shape
    xp = x_prompt.reshape(bp * tp, D_MODEL)
    xs = x_sample.reshape(bs * ts, D_MODEL)

    w_in = w_in_pm[0]
    w_main = w_in[:, :POOL_W + 4 * MLSTM_W].astype(BF16)
    n_gate = 2 * MLSTM_HEADS
    w_gate = jnp.pad(w_in[:, POOL_W + 4 * MLSTM_W:], ((0, 0), (0, LANES - n_gate))).astype(BF16)
    b_gate = jnp.pad(b_if[0], (0, LANES - n_gate))[None]
    w_pool_b = w_pool[0].astype(BF16)
    w_out = w_out_pm[0].astype(BF16)
    w_r, b_r = _route_args(w_router, b_router, 0)

    groups = (
        dict(x=xp, b=bp, t=tp, pos0=0, pool_bb=1, pool_tt=min(512, tp), l=MLSTM_CHUNK, conv_bb=1,
             conv_tt=min(256, tp),
             past_pool=jnp.zeros((bp, POOL_STATE, POOL_W), F32),
             c0=jnp.zeros((bp, MLSTM_HEADS, MLSTM_DH, MLSTM_DH), F32),
             n0=jnp.zeros((bp, MLSTM_HEADS, MLSTM_DH), F32),
             m0=jnp.zeros((bp, MLSTM_HEADS), F32),
             past_conv=jnp.zeros((bp, CONV_STATE, CONV_W), F32)),
        dict(x=xs, b=bs, t=ts, pos0=PAST_LEN, pool_bb=16, pool_tt=ts, l=ts, conv_bb=8, conv_tt=ts,
             past_pool=state_pool[0], c0=state_mlstm_C[0], n0=state_mlstm_n[0], m0=state_mlstm_m[0],
             past_conv=state_conv[0]),
    )

    x1, route, states = [], [], []
    carry = jnp.zeros((1, LANES), F32)
    for gr in groups:
        b, t = gr["b"], gr["t"]
        u, q, k, v, o, g = _pm_in_proj(gr["x"], w_main, w_gate, b_gate)
        three = lambda a: a.reshape(b, t, a.shape[-1])
        past = jnp.pad(gr["past_pool"], ((0, 0), (POOL_HALO - POOL_STATE, 0), (0, 0)))
        y_pool = _pool_mix(three(u), past, w_pool_b, pool_scale[0][None],
                           bb=gr["pool_bb"], tt=gr["pool_tt"], pos0=gr["pos0"])
        n0 = gr["n0"][:, :, None, :]
        m0 = jnp.broadcast_to(gr["m0"][:, :, None, None], (b, MLSTM_HEADS, 1, LANES))
        y_m, c_new, n_new, m_new = _mlstm(three(q), three(k), three(v), three(o), three(g), gr["c0"], n0, m0,
                                          hn_g[0][None], bb=8, l=gr["l"])
        x1_g, idx, rank, gate, carry = _mix_out(
            [y_pool.reshape(b * t, POOL_W), y_m.reshape(b * t, MLSTM_W)], [w_out[:POOL_W], w_out[POOL_W:]],
            gr["x"], b_out_pm[0][None], ln_mix_g[0][None], ln_mix_b[0][None], w_r, b_r, carry)
        x1.append(x1_g)
        route.append((idx, rank, gate))
        ext_pool = jnp.concatenate([gr["past_pool"], three(u)], axis=1)
        states.append((ext_pool[:, -POOL_STATE:], c_new, n_new[:, :, 0, :], m_new[:, :, 0, 0]))
    counts = carry[0, :N_EXPERTS].astype(I32)
    x2 = _moe_block(0, x1, (route[0], route[1], counts), ln_ffn_g, ln_ffn_b, w_gu, b_gu, w_dn, b_dn)

    w_cv = w_in_cv[0].astype(BF16)
    w_out = w_out_cv[0].astype(BF16)
    w_r, b_r = _route_args(w_router, b_router, 1)
    dw_w_p = jnp.pad(dw_w[0], ((0, CONV_HALO - CONV_K), (0, 0)))
    x3, route, conv_states = [], [], []
    carry = jnp.zeros((1, LANES), F32)
    for gr, x in zip(groups, x2):
        b, t = gr["b"], gr["t"]
        g = _cv_in_proj(x, w_cv, b_in_cv[0][None]).reshape(b, t, CONV_W)
        past = jnp.pad(gr["past_conv"], ((0, 0), (CONV_HALO - CONV_STATE, 0), (0, 0)))
        c = _dwconv(g, past, dw_w_p, dw_b[0][None], cln_g[0][None], cln_b[0][None],
                    bb=gr["conv_bb"], tt=gr["conv_tt"])
        x3_g, idx, rank, gate, carry = _mix_out(
            [c.reshape(b * t, CONV_W)], [w_out], x, b_out_cv[0][None], ln_mix_g[1][None], ln_mix_b[1][None],
            w_r, b_r, carry)
        x3.append(x3_g)
        route.append((idx, rank, gate))
        conv_states.append(jnp.concatenate([gr["past_conv"], g], axis=1)[:, -CONV_STATE:])
    counts = carry[0, :N_EXPERTS].astype(I32)
    y = _moe_block(1, x3, (route[0], route[1], counts), ln_ffn_g, ln_ffn_b, w_gu, b_gu, w_dn, b_dn)

    (pool_p, c_p, n_p, m_p), (pool_s, c_s, n_s, m_s) = states
    return (y[0].reshape(bp, tp, D_MODEL), y[1].reshape(bs, ts, D_MODEL),
            pool_p[None], pool_s[None], c_p[None], c_s[None], n_p[None], n_s[None], m_p[None], m_s[None],
            conv_states[0][None], conv_states[1][None])
```

```python
import functools

import jax
import jax.numpy as jnp
from jax import lax
from jax.experimental import pallas as pl
from jax.experimental.pallas import tpu as pltpu

F32 = jnp.float32
BF16 = jnp.bfloat16
I32 = jnp.int32

D_MODEL = 1024
DEPTH = 2
POOL_W = 512
POOL_WINDOWS = (2, 4, 8, 16)
POOL_GW = 128
POOL_STATE = 15
POOL_HALO = 16
MLSTM_HEADS = 4
MLSTM_W = 512
MLSTM_DH = 128
MLSTM_CHUNK = 64
CONV_W = 1024
CONV_K = 31
CONV_STATE = 30
CONV_HALO = 32
N_EXPERTS = 32
TOP_K = 4
D_FF = 1024
SWIGLU_LIMIT = 7.0
SWIGLU_ALPHA = 1.702
LN_EPS = 1e-5
DN_ALPHA = (2 * DEPTH) ** 0.25
PAST_LEN = 16384

LANES = 128
SUBLANES = 8
TOKEN_TILE = 512
MOE_TILE = 512
NEG_BIG = -1e30
VMEM_LIMIT = 56 * 1024 * 1024
HIGHEST = lax.Precision.HIGHEST


def _cparams(sem, vmem=VMEM_LIMIT):
    return pltpu.CompilerParams(dimension_semantics=sem, vmem_limit_bytes=vmem)


def _layer_norm(x, g, b):
    mu = jnp.mean(x, -1, keepdims=True)
    xc = x - mu
    var = jnp.mean(xc * xc, -1, keepdims=True)
    return xc * lax.rsqrt(var + LN_EPS) * g + b


def _log_sigmoid(x):
    return jnp.minimum(x, 0.0) - jnp.log1p(jnp.exp(-jnp.abs(x)))


def _dot(a, b):
    return jnp.dot(a, b, preferred_element_type=F32)


def _rows_of(cols):
    sel = (lax.broadcasted_iota(I32, (SUBLANES, LANES), 0) == lax.broadcasted_iota(I32, (SUBLANES, LANES), 1))
    return lax.dot_general(sel.astype(F32), cols, (((1,), (1,)), ((), ())),
                           precision=HIGHEST, preferred_element_type=F32)


_ANY = pl.BlockSpec(memory_space=pl.ANY)


def _full(a, grid_rank):
    return pl.BlockSpec(a.shape, lambda *_: (0,) * a.ndim)


def _pm_in_kernel(x_ref, w_ref, wg_ref, bg_ref, u_ref, q_ref, k_ref, v_ref, o_ref, g_ref):
    xb = x_ref[...].astype(BF16)
    for j, out in enumerate((u_ref, q_ref, k_ref, v_ref, o_ref)):
        out[...] = _dot(xb, w_ref[:, j * 512:(j + 1) * 512])
    g_ref[...] = _dot(xb, wg_ref[...]) + bg_ref[...]


def _pm_in_proj(x, w_main, w_gate, b_gate):
    n = x.shape[0]
    tm = TOKEN_TILE
    row = lambda c: pl.BlockSpec((tm, c), lambda i: (i, 0))
    return pl.pallas_call(
        _pm_in_kernel,
        grid=(n // tm,),
        in_specs=[row(D_MODEL), _full(w_main, 1), _full(w_gate, 1), _full(b_gate, 1)],
        out_specs=[row(512)] * 5 + [row(LANES)],
        out_shape=[jax.ShapeDtypeStruct((n, 512), F32)] * 5 + [jax.ShapeDtypeStruct((n, LANES), F32)],
        compiler_params=_cparams(("parallel",)),
        name="pm_in_proj",
    )(x, w_main, w_gate, b_gate)


def _seq_call(kernel, *, name, row0, b, t, bb, tt, n_rows, seq_ins, seq_widths, other_ins, other_specs,
              out_widths, extra_out_shapes, extra_out_specs, scratch, prev_outs):
    steps = t // tt
    blk0 = row0 // (bb * tt)
    seq_spec = lambda c: pl.BlockSpec((bb * tt, c), lambda i, j: (blk0 + i * steps + j, 0))
    prev = list(prev_outs)
    first_prev = len(seq_ins) + len(other_ins)
    aliases = {first_prev + o: o for o in range(len(out_widths))}
    return pl.pallas_call(
        functools.partial(kernel, n_prev=len(prev)),
        grid=(b // bb, steps),
        in_specs=[seq_spec(c) for c in seq_widths] + list(other_specs) + [_ANY] * len(prev),
        out_specs=[seq_spec(c) for c in out_widths] + list(extra_out_specs),
        out_shape=[jax.ShapeDtypeStruct((n_rows, c), F32) for c in out_widths] + list(extra_out_shapes),
        input_output_aliases=aliases,
        scratch_shapes=scratch,
        compiler_params=_cparams(("parallel", "arbitrary")),
        name=name,
    )(*seq_ins, *other_ins, *prev)


def _pool_kernel(u_ref, past_ref, w_ref, sc_ref, *rest, bb, tt, pos0, n_prev):
    y_ref, ext = rest[n_prev:]
    j = pl.program_id(1)

    @pl.when(j == 0)
    def _():
        ext[:, 0:POOL_HALO, :] = past_ref[...]

    u = u_ref[...].reshape(bb, tt, POOL_W)
    ext[:, POOL_HALO:POOL_HALO + tt, :] = u
    pos = pos0 + j * tt + lax.broadcasted_iota(I32, (tt, 1), 0)
    for g, w in enumerate(POOL_WINDOWS):
        lanes = slice(g * POOL_GW, (g + 1) * POOL_GW)
        wsum = ext[:, POOL_HALO:POOL_HALO + tt, lanes]
        for d in range(1, w):
            wsum = wsum + ext[:, POOL_HALO - d:POOL_HALO - d + tt, lanes]
        cnt = jnp.minimum(w, pos + 1).astype(F32)
        pooled = wsum / cnt[None] - u[:, :, lanes]
        y = _dot(pooled.reshape(bb * tt, POOL_GW).astype(BF16), w_ref[g])
        y_ref[:, lanes] = y * sc_ref[:, lanes]
    ext[:, 0:POOL_HALO, :] = ext[:, tt:tt + POOL_HALO, :]


def _pool_mix(u, past, w_pool, scale, *, row0, b, t, bb, tt, pos0, prev):
    return _seq_call(
        functools.partial(_pool_kernel, bb=bb, tt=tt, pos0=pos0), name="pool_mix",
        row0=row0, b=b, t=t, bb=bb, tt=tt, n_rows=u.shape[0],
        seq_ins=[u], seq_widths=[POOL_W],
        other_ins=[past, w_pool, scale],
        other_specs=[pl.BlockSpec((bb, POOL_HALO, POOL_W), lambda i, j: (i, 0, 0)), _full(w_pool, 2), _full(scale, 2)],
        out_widths=[POOL_W], extra_out_shapes=[], extra_out_specs=[],
        scratch=[pltpu.VMEM((bb, POOL_HALO + tt, POOL_W), F32)], prev_outs=prev)[0]


def _mlstm_kernel(q_ref, k_ref, v_ref, o_ref, g_ref, c0_ref, n0_ref, m0_ref, hng_ref, *rest, bb, l, n_prev):
    y_ref, c_out, n_out, m_out, cs, ns, ms = rest[n_prev:]
    c = pl.program_id(1)

    @pl.when(c == 0)
    def _():
        cs[...] = c0_ref[...]
        ns[...] = n0_ref[...]
        ms[...] = m0_ref[...]

    row = lax.broadcasted_iota(I32, (l, l), 0)
    col = lax.broadcasted_iota(I32, (l, l), 1)
    causal = row >= col
    tril = causal.astype(F32)
    lane = lax.broadcasted_iota(I32, (l, LANES), 1)
    scale = MLSTM_DH ** -0.5
    padr = LANES - l if l < 16 else 0

    def per_seq(b, carry):
        rows_b = pl.ds(pl.multiple_of(b * l, SUBLANES), l)
        gates = g_ref[rows_b, :]
        logf = _log_sigmoid(gates)
        bcum = jnp.dot(tril, logf, precision=HIGHEST, preferred_element_type=F32)
        cols = jnp.where(lane < MLSTM_HEADS, gates, bcum)
        rows = _rows_of(cols)
        for h in range(MLSTM_HEADS):
            hl = slice(h * MLSTM_DH, (h + 1) * MLSTM_DH)
            i_col = cols[:, h:h + 1]
            b_col = cols[:, MLSTM_HEADS + h:MLSTM_HEADS + h + 1]
            i_row = rows[h:h + 1, :]
            b_row = rows[MLSTM_HEADS + h:MLSTM_HEADS + h + 1, :]
            m_prev = ms[b, h][:, 0:1]
            c_prev = cs[b, h]
            n_prev_ = ns[b, h]
            qh = q_ref[rows_b, hl] * scale
            kh = k_ref[rows_b, hl]
            vh = v_ref[rows_b, hl]

            dmat = jnp.where(causal, b_col - b_row + i_row, -jnp.inf)
            inter = b_col + m_prev
            m_t = jnp.maximum(inter, jnp.max(dmat, axis=-1, keepdims=True))
            qk = lax.dot_general(qh, kh, (((1,), (1,)), ((), ())), preferred_element_type=F32)
            s = qk * jnp.exp(dmat - m_t)
            w_inter = jnp.exp(inter - m_t)
            num = _dot(s, vh) + w_inter * _dot(qh, c_prev)
            den = jnp.sum(s, -1, keepdims=True) + w_inter * jnp.sum(qh * n_prev_, -1, keepdims=True)
            bound = jnp.maximum(jnp.abs(den), jnp.exp(-m_t))
            hh = num / bound

            b_end = b_col[l - 1:l, :]
            gcol = b_end - b_col + i_col
            m_new = jnp.maximum(b_end + m_prev, jnp.max(gcol, axis=0, keepdims=True))
            kw = kh * jnp.exp(gcol - m_new)
            decay = jnp.exp(b_end + m_prev - m_new)
            if padr:
                zpad = jnp.zeros((padr, MLSTM_DH), F32)
                kv = lax.dot_general(jnp.concatenate([kw, zpad], 0), jnp.concatenate([vh, zpad], 0),
                                     (((0,), (0,)), ((), ())), preferred_element_type=F32)
            else:
                kv = lax.dot_general(kw, vh, (((0,), (0,)), ((), ())), preferred_element_type=F32)
            cs[b, h] = decay * c_prev + kv
            ns[b, h] = decay * n_prev_ + jnp.sum(kw, axis=0, keepdims=True)
            ms[b, h] = jnp.broadcast_to(m_new, (1, LANES))

            mu = jnp.mean(hh, -1, keepdims=True)
            hc = hh - mu
            var = jnp.mean(hc * hc, -1, keepdims=True)
            hn = hc * lax.rsqrt(var + LN_EPS) * hng_ref[:, hl]
            y_ref[rows_b, hl] = jax.nn.sigmoid(o_ref[rows_b, hl]) * hn
        return carry

    lax.fori_loop(0, bb, per_seq, 0)

    @pl.when(c == pl.num_programs(1) - 1)
    def _():
        c_out[...] = cs[...]
        n_out[...] = ns[...]
        m_out[...] = ms[...]


def _mlstm(q, k, v, o, g, c0, n0, m0, hn_g, *, row0, b, t, bb, l, prev):
    st_c = pl.BlockSpec((bb, MLSTM_HEADS, MLSTM_DH, MLSTM_DH), lambda i, j: (i, 0, 0, 0))
    st_v = pl.BlockSpec((bb, MLSTM_HEADS, 1, LANES), lambda i, j: (i, 0, 0, 0))
    return _seq_call(
        functools.partial(_mlstm_kernel, bb=bb, l=l), name="mlstm",
        row0=row0, b=b, t=t, bb=bb, tt=l, n_rows=q.shape[0],
        seq_ins=[q, k, v, o, g], seq_widths=[MLSTM_W] * 4 + [LANES],
        other_ins=[c0, n0, m0, hn_g], other_specs=[st_c, st_v, st_v, _full(hn_g, 2)],
        out_widths=[MLSTM_W],
        extra_out_shapes=[jax.ShapeDtypeStruct((b, MLSTM_HEADS, MLSTM_DH, MLSTM_DH), F32),
                          jax.ShapeDtypeStruct((b, MLSTM_HEADS, 1, LANES), F32),
                          jax.ShapeDtypeStruct((b, MLSTM_HEADS, 1, LANES), F32)],
        extra_out_specs=[st_c, st_v, st_v],
        scratch=[pltpu.VMEM((bb, MLSTM_HEADS, MLSTM_DH, MLSTM_DH), F32),
                 pltpu.VMEM((bb, MLSTM_HEADS, 1, LANES), F32),
                 pltpu.VMEM((bb, MLSTM_HEADS, 1, LANES), F32)],
        prev_outs=prev)


def _mix_out_kernel(*refs, n_parts):
    parts = refs[:n_parts]
    ws = refs[n_parts:2 * n_parts]
    (x_ref, bo_ref, g_ref, b_ref, wr_ref, br_ref,
     x1_ref, route_ref, gate_ref, cnt_ref, carry) = refs[2 * n_parts:]
    i = pl.program_id(0)

    @pl.when(i == 0)
    def _():
        carry[...] = jnp.zeros_like(carry)

    mix = bo_ref[...]
    for a_ref, w_ref in zip(parts, ws):
        mix = mix + _dot(a_ref[...].astype(BF16), w_ref[...])
    x1 = _layer_norm(DN_ALPHA * x_ref[...] + mix, g_ref[...], b_ref[...])
    x1_ref[...] = x1

    tm = x1.shape[0]
    logits = _dot(x1.astype(BF16), wr_ref[...]) + br_ref[...]
    lane = lax.broadcasted_iota(I32, (tm, LANES), 1)
    work = logits
    picks, vals = [], []
    onehot = jnp.zeros((tm, LANES), F32)
    for _ in range(TOP_K):
        mx = jnp.max(work, axis=-1, keepdims=True)
        pick = jnp.min(jnp.where(work == mx, lane, LANES), axis=-1, keepdims=True)
        hit = lane == pick
        onehot = jnp.where(hit, 1.0, onehot)
        work = jnp.where(hit, -jnp.inf, work)
        picks.append(pick)
        vals.append(mx)
    exps = [jnp.exp(v - vals[0]) for v in vals]
    tot = exps[0] + exps[1] + exps[2] + exps[3]

    rr = lax.broadcasted_iota(I32, (tm, tm), 0)
    cc = lax.broadcasted_iota(I32, (tm, tm), 1)
    before = _dot((rr > cc).astype(BF16), onehot.astype(BF16)) + carry[...]
    route = jnp.zeros((tm, LANES), F32)
    gate_o = jnp.zeros((tm, LANES), F32)
    for kk in range(TOP_K):
        rk = jnp.sum(jnp.where(lane == picks[kk], before, 0.0), axis=-1, keepdims=True)
        route = jnp.where(lane == kk, picks[kk].astype(F32), route)
        route = jnp.where(lane == TOP_K + kk, rk, route)
        gate_o = jnp.where(lane == kk, exps[kk] / tot, gate_o)
    route_ref[...] = _rows_of(route).astype(I32)
    gate_ref[...] = gate_o
    carry[...] = carry[...] + jnp.sum(onehot, axis=0, keepdims=True)
    cnt_ref[...] = carry[...]


def _mix_out(parts, ws, x, b_out, ln_g, ln_b, w_r, b_r):
    n = x.shape[0]
    tm = TOKEN_TILE
    row = lambda c: pl.BlockSpec((tm, c), lambda i: (i, 0))
    small = [b_out, ln_g, ln_b, w_r, b_r]
    return pl.pallas_call(
        functools.partial(_mix_out_kernel, n_parts=len(parts)),
        grid=(n // tm,),
        in_specs=[row(p.shape[1]) for p in parts] + [_full(w, 1) for w in ws] + [row(D_MODEL)]
        + [_full(a, 1) for a in small],
        out_specs=[row(D_MODEL), pl.BlockSpec((SUBLANES, tm), lambda i: (0, i)), row(LANES),
                   pl.BlockSpec((1, LANES), lambda i: (0, 0))],
        out_shape=[
            jax.ShapeDtypeStruct((n, D_MODEL), F32),
            jax.ShapeDtypeStruct((SUBLANES, n), I32),
            jax.ShapeDtypeStruct((n, LANES), F32),
            jax.ShapeDtypeStruct((1, LANES), F32),
        ],
        scratch_shapes=[pltpu.VMEM((1, LANES), F32)],
        compiler_params=_cparams(("arbitrary",)),
        name="mix_out_router",
    )(*parts, *ws, x, *small)


def _moe_kernel(w_tile, w_exp, w_lo, w_hi, w_first, w_last, w_newexp, w_valid,
                tab_hbm, x_hbm, wgu_ref, bgu_ref, wdn_ref, bdn_ref, yk_hbm,
                wgu_b, wdn_b, xbuf, ybuf, tab, tsem, gsem, ssem, *, n_tiles, tm):
    j = pl.program_id(0)
    t = w_tile[j]
    b2 = lax.rem(t, 2)

    groups = tm // SUBLANES

    def tab_copy(tile):
        b3 = lax.rem(tile, 3)
        return pltpu.make_async_copy(tab_hbm.at[pl.ds(tile * (2 * tm), 2 * tm)],
                                     tab.at[pl.ds(b3 * (2 * tm), 2 * tm)], tsem.at[b3])

    def hbm_row(ref, row):
        return ref.at[lax.shift_right_logical(row, 3), pl.ds(row & (SUBLANES - 1), 1), :]

    def start_gather(tile):
        base = lax.rem(tile, 3) * (2 * tm)
        bx = lax.rem(tile, 2)

        def body(i, carry):
            for u in range(SUBLANES):
                src = tab[base + i * SUBLANES + u]
                pltpu.make_async_copy(hbm_row(x_hbm, src), xbuf.at[bx, i, pl.ds(u, 1), :], gsem.at[bx]).start()
            return carry

        lax.fori_loop(0, groups, body, 0)

    def start_scatter(tile):
        base = lax.rem(tile, 3) * (2 * tm) + tm
        by = lax.rem(tile, 2)

        def body(i, carry):
            for u in range(SUBLANES):
                dst = tab[base + i * SUBLANES + u]
                pltpu.make_async_copy(ybuf.at[by, i, pl.ds(u, 1), :], hbm_row(yk_hbm, dst), ssem.at[by]).start()
            return carry

        lax.fori_loop(0, groups, body, 0)

    def wait_gather(tile):
        bx = lax.rem(tile, 2)
        pltpu.make_async_copy(x_hbm.at[pl.ds(0, groups)], xbuf.at[bx], gsem.at[bx]).wait()

    def wait_scatter(tile):
        by = lax.rem(tile, 2)
        pltpu.make_async_copy(ybuf.at[by], yk_hbm.at[pl.ds(0, groups)], ssem.at[by]).wait()

    @pl.when(j == 0)
    def _():
        first_tab = tab_copy(0)
        first_tab.start()
        first_tab.wait()
        start_gather(0)
        tab_copy(1).start()

    @pl.when(w_first[j] == 1)
    def _():
        wait_gather(t)

        @pl.when(t + 1 < n_tiles)
        def _():
            tab_copy(t + 1).wait()
            start_gather(t + 1)

        @pl.when(t + 2 < n_tiles)
        def _():
            tab_copy(t + 2).start()

        @pl.when(t >= 2)
        def _():
            wait_scatter(t - 2)

        ybuf[b2] = jnp.zeros((groups, SUBLANES, D_MODEL), F32)

    @pl.when(w_newexp[j] == 1)
    def _():
        wgu_b[...] = wgu_ref[0].astype(BF16)
        wdn_b[...] = wdn_ref[0].astype(BF16)

    @pl.when(w_valid[j] == 1)
    def _():
        gu = _dot(xbuf[b2].reshape(tm, D_MODEL).astype(BF16), wgu_b[...]) + bgu_ref[0]
        g = jnp.minimum(gu[:, :D_FF], SWIGLU_LIMIT)
        u = jnp.clip(gu[:, D_FF:], -SWIGLU_LIMIT, SWIGLU_LIMIT)
        h = (u + 1.0) * g * jax.nn.sigmoid(SWIGLU_ALPHA * g)
        y = _dot(h.astype(BF16), wdn_b[...]) + bdn_ref[0]
        r = lax.broadcasted_iota(I32, (tm, 1), 0)
        mine = (r >= w_lo[j]) & (r < w_hi[j])
        y = jnp.where(mine, y, ybuf[b2].reshape(tm, D_MODEL))
        ybuf[b2] = y.reshape(groups, SUBLANES, D_MODEL)

    @pl.when(w_last[j] == 1)
    def _():
        start_scatter(t)

        @pl.when(t == n_tiles - 1)
        def _():
            wait_scatter(t - 1)
            wait_scatter(t)


def _moe_experts(work, tab, x1, w_gu, b_gu, w_dn, b_dn):
    n = x1.shape[0]
    tm = MOE_TILE
    n_tiles = n * TOP_K // tm
    n_work = work[0].shape[0]
    by_expert = lambda j, wt, we, *_: (we[j], 0, 0)
    grid_spec = pltpu.PrefetchScalarGridSpec(
        num_scalar_prefetch=8,
        grid=(n_work,),
        in_specs=[
            _ANY, _ANY,
            pl.BlockSpec((1, D_MODEL, 2 * D_FF), by_expert),
            pl.BlockSpec((1, 1, 2 * D_FF), by_expert),
            pl.BlockSpec((1, D_FF, D_MODEL), by_expert),
            pl.BlockSpec((1, 1, D_MODEL), by_expert),
        ],
        out_specs=_ANY,
        scratch_shapes=[
            pltpu.VMEM((D_MODEL, 2 * D_FF), BF16), pltpu.VMEM((D_FF, D_MODEL), BF16),
            pltpu.VMEM((2, tm // SUBLANES, SUBLANES, D_MODEL), F32),
            pltpu.VMEM((2, tm // SUBLANES, SUBLANES, D_MODEL), F32),
            pltpu.SMEM((3 * 2 * tm,), I32),
            pltpu.SemaphoreType.DMA((3,)), pltpu.SemaphoreType.DMA((2,)), pltpu.SemaphoreType.DMA((2,)),
        ],
    )
    return pl.pallas_call(
        functools.partial(_moe_kernel, n_tiles=n_tiles, tm=tm),
        grid_spec=grid_spec,
        out_shape=jax.ShapeDtypeStruct((TOP_K * n // SUBLANES, SUBLANES, D_MODEL), F32),
        compiler_params=_cparams(("arbitrary",)),
        name="moe_experts",
    )(*work, tab, x1.reshape(n // SUBLANES, SUBLANES, D_MODEL), w_gu, b_gu, w_dn, b_dn)


def _moe_work_items(counts, n_slots):
    tm = MOE_TILE
    n_tiles = n_slots // tm
    n_work = n_tiles + N_EXPERTS
    ends = jnp.cumsum(counts)
    offs = ends - counts
    cand = jnp.concatenate([jnp.arange(n_tiles, dtype=I32) * tm, offs])
    ci = jnp.arange(n_work, dtype=I32)
    less = (cand[None, :] < cand[:, None]) | ((cand[None, :] == cand[:, None]) & (ci[None, :] < ci[:, None]))
    order = jnp.sum(less.astype(I32), axis=1)
    starts = jnp.sum(jnp.where(order[None, :] == ci[:, None], cand[None, :], 0), axis=1)
    stops = jnp.concatenate([starts[1:], jnp.array([n_slots], I32)])
    valid = stops > starts
    vpos = jnp.cumsum(valid.astype(I32)) - 1
    n_valid = jnp.sum(valid.astype(I32))
    want = jnp.minimum(ci, n_valid - 1)
    pick = valid[None, :] & (vpos[None, :] == want[:, None])
    starts = jnp.sum(jnp.where(pick, starts[None, :], 0), axis=1)
    stops = jnp.sum(jnp.where(pick, stops[None, :], 0), axis=1)
    is_valid = ci < n_valid
    tile = starts // tm
    expert = jnp.minimum(jnp.sum((ends[None, :] <= starts[:, None]).astype(I32), axis=1), N_EXPERTS - 1)
    prev = lambda a: jnp.concatenate([jnp.array([-1], I32), a[:-1]])
    nxt = lambda a: jnp.concatenate([a[1:], jnp.array([-1], I32)])
    first = (tile != prev(tile)) & is_valid
    last = ((tile != nxt(tile)) | (ci == n_valid - 1)) & is_valid
    newexp = (expert != prev(expert)) & is_valid
    as_i = lambda a: a.astype(I32)
    return offs, (tile, expert, starts - tile * tm, stops - tile * tm, as_i(first), as_i(last), as_i(newexp),
                  as_i(is_valid))


def _ffn_ln_kernel(x_ref, yk_ref, gate_ref, g_ref, b_ref, out_ref):
    gate = gate_ref[...]
    ff = gate[:, 0:1] * yk_ref[0]
    for kk in range(1, TOP_K):
        ff = ff + gate[:, kk:kk + 1] * yk_ref[kk]
    out_ref[...] = _layer_norm(DN_ALPHA * x_ref[...] + ff, g_ref[...], b_ref[...])


def _ffn_ln(x, yk, gate, ln_g, ln_b):
    n = x.shape[0]
    tm = TOKEN_TILE
    row = lambda c: pl.BlockSpec((tm, c), lambda i: (i, 0))
    vec = pl.BlockSpec((1, D_MODEL), lambda i: (0, 0))
    return pl.pallas_call(
        _ffn_ln_kernel,
        grid=(n // tm,),
        in_specs=[row(D_MODEL), pl.BlockSpec((TOP_K, tm, D_MODEL), lambda i: (0, i, 0)), row(LANES), vec, vec],
        out_specs=row(D_MODEL),
        out_shape=jax.ShapeDtypeStruct((n, D_MODEL), F32),
        compiler_params=_cparams(("parallel",)),
        name="ffn_ln",
    )(x, yk, gate, ln_g, ln_b)


def _moe_block(l, x1, route_t, gate, counts, ln_g, ln_b, w_gu, b_gu, w_dn, b_dn):
    n = x1.shape[0]
    n_slots = n * TOP_K
    tm = MOE_TILE
    offs, work = _moe_work_items(counts, n_slots)
    idx_t, rank_t = route_t[:TOP_K], route_t[TOP_K:]
    experts = jnp.arange(N_EXPERTS, dtype=I32)
    slot_t = rank_t + jnp.sum(jnp.where(idx_t[..., None] == experts, offs, 0), axis=-1)
    pair = jnp.argsort(slot_t.reshape(-1)).astype(I32)
    tab = jnp.concatenate([(pair % n).reshape(-1, tm), pair.reshape(-1, tm)], axis=1).reshape(-1)
    yk = _moe_experts(work, tab, x1, w_gu[l], b_gu[l][:, None, :], w_dn[l], b_dn[l][:, None, :])
    return _ffn_ln(x1, yk.reshape(TOP_K, n, D_MODEL), gate, ln_g[l][None], ln_b[l][None])


def _cv_in_kernel(x_ref, w_ref, b_ref, g_ref):
    xb = x_ref[...].astype(BF16)
    a = _dot(xb, w_ref[:, :CONV_W]) + b_ref[:, :CONV_W]
    gate = _dot(xb, w_ref[:, CONV_W:]) + b_ref[:, CONV_W:]
    g_ref[...] = a * jax.nn.sigmoid(gate)


def _cv_in_proj(x, w, b):
    n = x.shape[0]
    tm = TOKEN_TILE
    row = lambda c: pl.BlockSpec((tm, c), lambda i: (i, 0))
    return pl.pallas_call(
        _cv_in_kernel,
        grid=(n // tm,),
        in_specs=[row(D_MODEL), _full(w, 1), _full(b, 1)],
        out_specs=row(CONV_W),
        out_shape=jax.ShapeDtypeStruct((n, CONV_W), F32),
        compiler_params=_cparams(("parallel",)),
        name="cv_in_proj",
    )(x, w, b)


def _dwconv_kernel(g_ref, past_ref, w_ref, wb_ref, lg_ref, lb_ref, *rest, bb, tt, n_prev):
    c_ref, ext = rest[n_prev:]
    j = pl.program_id(1)

    @pl.when(j == 0)
    def _():
        ext[:, 0:CONV_HALO, :] = past_ref[...]

    ext[:, CONV_HALO:CONV_HALO + tt, :] = g_ref[...].reshape(bb, tt, CONV_W)
    base = CONV_HALO - CONV_STATE

    def per_lane_tile(ci, carry):
        lanes = pl.ds(pl.multiple_of(ci * LANES, LANES), LANES)
        acc = ext[:, base:base + tt, lanes] * w_ref[0:1, lanes][None]
        for d in range(1, CONV_K):
            acc = acc + ext[:, base + d:base + d + tt, lanes] * w_ref[d:d + 1, lanes][None]
        c_ref[:, lanes] = acc.reshape(bb * tt, LANES) + wb_ref[:, lanes]
        return carry

    lax.fori_loop(0, CONV_W // LANES, per_lane_tile, 0)
    c = _layer_norm(c_ref[...], lg_ref[...], lb_ref[...])
    c_ref[...] = c * jax.nn.sigmoid(c)
    ext[:, 0:CONV_HALO, :] = ext[:, tt:tt + CONV_HALO, :]


def _dwconv(g, past, dw_w, dw_b, ln_g, ln_b, *, row0, b, t, bb, tt, prev):
    vec = pl.BlockSpec((1, CONV_W), lambda i, j: (0, 0))
    return _seq_call(
        functools.partial(_dwconv_kernel, bb=bb, tt=tt), name="dwconv_ln_swish",
        row0=row0, b=b, t=t, bb=bb, tt=tt, n_rows=g.shape[0],
        seq_ins=[g], seq_widths=[CONV_W],
        other_ins=[past, dw_w, dw_b, ln_g, ln_b],
        other_specs=[pl.BlockSpec((bb, CONV_HALO, CONV_W), lambda i, j: (i, 0, 0)), _full(dw_w, 2), vec, vec, vec],
        out_widths=[CONV_W], extra_out_shapes=[], extra_out_specs=[],
        scratch=[pltpu.VMEM((bb, CONV_HALO + tt, CONV_W), F32)], prev_outs=prev)[0]


def _route_args(w_router, b_router, l):
    w_r = jnp.pad(w_router[l], ((0, 0), (0, LANES - N_EXPERTS))).astype(BF16)
    b_r = jnp.pad(b_router[l], (0, LANES - N_EXPERTS), constant_values=NEG_BIG)[None]
    return w_r, b_r


def kernel(x_prompt, x_sample, state_pool, state_mlstm_C, state_mlstm_n, state_mlstm_m, state_conv, w_in_pm, b_if, w_pool, pool_scale, hn_g, w_out_pm, b_out_pm, w_in_cv, b_in_cv, dw_w, dw_b, cln_g, cln_b, w_out_cv, b_out_cv, ln_mix_g, ln_mix_b, ln_ffn_g, ln_ffn_b, w_router, b_router, w_gu, b_gu, w_dn, b_dn):
    bp, tp, _ = x_prompt.shape
    bs, ts, _ = x_sample.shape
    n_p = bp * tp
    x0 = jnp.concatenate([x_prompt.reshape(n_p, D_MODEL), x_sample.reshape(bs * ts, D_MODEL)], axis=0)

    groups = (
        dict(row0=0, b=bp, t=tp, pos0=0, pool_bb=1, pool_tt=min(512, tp), l=MLSTM_CHUNK, mlstm_bb=1, conv_bb=1,
             conv_tt=min(256, tp),
             past_pool=jnp.zeros((bp, POOL_STATE, POOL_W), F32),
             c0=jnp.zeros((bp, MLSTM_HEADS, MLSTM_DH, MLSTM_DH), F32),
             n0=jnp.zeros((bp, MLSTM_HEADS, MLSTM_DH), F32),
             m0=jnp.zeros((bp, MLSTM_HEADS), F32),
             past_conv=jnp.zeros((bp, CONV_STATE, CONV_W), F32)),
        dict(row0=n_p, b=bs, t=ts, pos0=PAST_LEN, pool_bb=16, pool_tt=ts, l=ts, mlstm_bb=8, conv_bb=8, conv_tt=ts,
             past_pool=state_pool[0], c0=state_mlstm_C[0], n0=state_mlstm_n[0], m0=state_mlstm_m[0],
             past_conv=state_conv[0]),
    )
    seq_view = lambda a, gr: a[gr["row0"]:gr["row0"] + gr["b"] * gr["t"]].reshape(gr["b"], gr["t"], a.shape[-1])

    w_in = w_in_pm[0]
    w_main = w_in[:, :POOL_W + 4 * MLSTM_W].astype(BF16)
    n_gate = 2 * MLSTM_HEADS
    w_gate = jnp.pad(w_in[:, POOL_W + 4 * MLSTM_W:], ((0, 0), (0, LANES - n_gate))).astype(BF16)
    b_gate = jnp.pad(b_if[0], (0, LANES - n_gate))[None]
    w_out = w_out_pm[0].astype(BF16)
    w_r, b_r = _route_args(w_router, b_router, 0)

    u, q, k, v, o, g = _pm_in_proj(x0, w_main, w_gate, b_gate)
    y_pool = jnp.zeros((x0.shape[0], POOL_W), F32)
    y_m = jnp.zeros((x0.shape[0], MLSTM_W), F32)
    states = []
    for gr in groups:
        b, t = gr["b"], gr["t"]
        past = jnp.pad(gr["past_pool"], ((0, 0), (POOL_HALO - POOL_STATE, 0), (0, 0)))
        y_pool = _pool_mix(u, past, w_pool[0].astype(BF16), pool_scale[0][None], row0=gr["row0"], b=b, t=t,
                           bb=gr["pool_bb"], tt=gr["pool_tt"], pos0=gr["pos0"],
                           prev=[y_pool])
        n0 = gr["n0"][:, :, None, :]
        m0 = jnp.broadcast_to(gr["m0"][:, :, None, None], (b, MLSTM_HEADS, 1, LANES))
        y_m, c_new, n_new, m_new = _mlstm(q, k, v, o, g, gr["c0"], n0, m0, hn_g[0][None], row0=gr["row0"], b=b, t=t,
                                          bb=gr["mlstm_bb"], l=gr["l"], prev=[y_m])
        ext_pool = jnp.concatenate([gr["past_pool"], seq_view(u, gr)], axis=1)
        states.append((ext_pool[:, -POOL_STATE:], c_new, n_new[:, :, 0, :], m_new[:, :, 0, 0]))
    x1, route_t, gate, counts = _mix_out([y_pool, y_m], [w_out[:POOL_W], w_out[POOL_W:]], x0, b_out_pm[0][None],
                                         ln_mix_g[0][None], ln_mix_b[0][None], w_r, b_r)
    x2 = _moe_block(0, x1, route_t, gate, counts[0, :N_EXPERTS].astype(I32), ln_ffn_g, ln_ffn_b,
                    w_gu, b_gu, w_dn, b_dn)

    w_r, b_r = _route_args(w_router, b_router, 1)
    dw_w_p = jnp.pad(dw_w[0], ((0, CONV_HALO - CONV_K), (0, 0)))
    gl = _cv_in_proj(x2, w_in_cv[0].astype(BF16), b_in_cv[0][None])
    c, conv_states = jnp.zeros((x0.shape[0], CONV_W), F32), []
    for gr in groups:
        past = jnp.pad(gr["past_conv"], ((0, 0), (CONV_HALO - CONV_STATE, 0), (0, 0)))
        c = _dwconv(gl, past, dw_w_p, dw_b[0][None], cln_g[0][None], cln_b[0][None], row0=gr["row0"], b=gr["b"],
                    t=gr["t"], bb=gr["conv_bb"], tt=gr["conv_tt"], prev=[c])
        conv_states.append(jnp.concatenate([gr["past_conv"], seq_view(gl, gr)], axis=1)[:, -CONV_STATE:])
    x3, route_t, gate, counts = _mix_out([c], [w_out_cv[0].astype(BF16)], x2, b_out_cv[0][None],
                                         ln_mix_g[1][None], ln_mix_b[1][None], w_r, b_r)
    y = _moe_block(1, x3, route_t, gate, counts[0, :N_EXPERTS].astype(I32), ln_ffn_g, ln_ffn_b,
                   w_gu, b_gu, w_dn, b_dn)

    (pool_p, c_p, n_p_, m_p), (pool_s, c_s, n_s, m_s) = states
    return (y[:n_p].reshape(bp, tp, D_MODEL), y[n_p:].reshape(bs, ts, D_MODEL),
            pool_p[None], pool_s[None], c_p[None], c_s[None], n_p_[None], n_s[None], m_p[None], m_s[None],
            conv_states[0][None], conv_states[1][None])
```

```python
import functools

import jax
import jax.numpy as jnp
from jax import lax
from jax.experimental import pallas as pl
from jax.experimental.pallas import tpu as pltpu

F32 = jnp.float32
BF16 = jnp.bfloat16
I32 = jnp.int32

D_MODEL = 1024
DEPTH = 2
POOL_W = 512
POOL_WINDOWS = (2, 4, 8, 16)
POOL_GW = 128
POOL_STATE = 15
POOL_HALO = 16
MLSTM_HEADS = 4
MLSTM_W = 512
MLSTM_DH = 128
MLSTM_CHUNK = 64
CONV_W = 1024
CONV_K = 31
CONV_STATE = 30
CONV_HALO = 32
N_EXPERTS = 32
TOP_K = 4
D_FF = 1024
SWIGLU_LIMIT = 7.0
SWIGLU_ALPHA = 1.702
LN_EPS = 1e-5
DN_ALPHA = (2 * DEPTH) ** 0.25
PAST_LEN = 16384

LANES = 128
SUBLANES = 8
TOKEN_TILE = 512
MOE_TILE = 512
NEG_BIG = -1e30
VMEM_LIMIT = 56 * 1024 * 1024
HIGHEST = lax.Precision.HIGHEST


def _cparams(sem, vmem=VMEM_LIMIT):
    return pltpu.CompilerParams(dimension_semantics=sem, vmem_limit_bytes=vmem)


def _layer_norm(x, g, b):
    mu = jnp.mean(x, -1, keepdims=True)
    xc = x - mu
    var = jnp.mean(xc * xc, -1, keepdims=True)
    return xc * lax.rsqrt(var + LN_EPS) * g + b


def _log_sigmoid(x):
    return jnp.minimum(x, 0.0) - jnp.log1p(jnp.exp(-jnp.abs(x)))


def _dot(a, b):
    return jnp.dot(a, b, preferred_element_type=F32)


def _rows_of(cols):
    sel = (lax.broadcasted_iota(I32, (SUBLANES, LANES), 0) == lax.broadcasted_iota(I32, (SUBLANES, LANES), 1))
    return lax.dot_general(sel.astype(F32), cols, (((1,), (1,)), ((), ())),
                           precision=HIGHEST, preferred_element_type=F32)


TILE_ROWS = D_MODEL // LANES


def _store_token_tiles(ref, val):
    tokens = val.shape[0]
    for s in range(TILE_ROWS):
        ref[pl.ds(s, tokens, stride=TILE_ROWS), :] = val[:, s * LANES:(s + 1) * LANES]


def _load_token_tiles(ref, tokens, dtype=F32):
    return jnp.concatenate([ref[pl.ds(s, tokens, stride=TILE_ROWS), :].astype(dtype) for s in range(TILE_ROWS)],
                           axis=1)


_ANY = pl.BlockSpec(memory_space=pl.ANY)


def _full(a, grid_rank):
    return pl.BlockSpec(a.shape, lambda *_: (0,) * a.ndim)


def _pm_in_kernel(x_ref, w_ref, wg_ref, bg_ref, u_ref, q_ref, k_ref, v_ref, o_ref, g_ref):
    xb = x_ref[...].astype(BF16)
    for j, out in enumerate((u_ref, q_ref, k_ref, v_ref, o_ref)):
        out[...] = _dot(xb, w_ref[:, j * 512:(j + 1) * 512])
    g_ref[...] = _dot(xb, wg_ref[...]) + bg_ref[...]


def _pm_in_proj(x, w_main, w_gate, b_gate):
    n = x.shape[0]
    tm = TOKEN_TILE
    row = lambda c: pl.BlockSpec((tm, c), lambda i: (i, 0))
    return pl.pallas_call(
        _pm_in_kernel,
        grid=(n // tm,),
        in_specs=[row(D_MODEL), _full(w_main, 1), _full(w_gate, 1), _full(b_gate, 1)],
        out_specs=[row(512)] * 5 + [row(LANES)],
        out_shape=[jax.ShapeDtypeStruct((n, 512), F32)] * 5 + [jax.ShapeDtypeStruct((n, LANES), F32)],
        compiler_params=_cparams(("parallel",)),
        name="pm_in_proj",
    )(x, w_main, w_gate, b_gate)


def _seq_call(kernel, *, name, row0, b, t, bb, tt, n_rows, seq_ins, seq_widths, other_ins, other_specs,
              out_widths, extra_out_shapes, extra_out_specs, scratch, prev_outs):
    steps = t // tt
    blk0 = row0 // (bb * tt)
    seq_spec = lambda c: pl.BlockSpec((bb * tt, c), lambda i, j: (blk0 + i * steps + j, 0))
    prev = list(prev_outs)
    first_prev = len(seq_ins) + len(other_ins)
    aliases = {first_prev + o: o for o in range(len(out_widths))}
    return pl.pallas_call(
        functools.partial(kernel, n_prev=len(prev)),
        grid=(b // bb, steps),
        in_specs=[seq_spec(c) for c in seq_widths] + list(other_specs) + [_ANY] * len(prev),
        out_specs=[seq_spec(c) for c in out_widths] + list(extra_out_specs),
        out_shape=[jax.ShapeDtypeStruct((n_rows, c), F32) for c in out_widths] + list(extra_out_shapes),
        input_output_aliases=aliases,
        scratch_shapes=scratch,
        compiler_params=_cparams(("parallel", "arbitrary")),
        name=name,
    )(*seq_ins, *other_ins, *prev)


def _pool_kernel(u_ref, past_ref, w_ref, sc_ref, *rest, bb, tt, pos0, n_prev):
    y_ref, ext = rest[n_prev:]
    j = pl.program_id(1)

    @pl.when(j == 0)
    def _():
        ext[:, 0:POOL_HALO, :] = past_ref[...]

    u = u_ref[...].reshape(bb, tt, POOL_W)
    ext[:, POOL_HALO:POOL_HALO + tt, :] = u
    pos = pos0 + j * tt + lax.broadcasted_iota(I32, (tt, 1), 0)
    for g, w in enumerate(POOL_WINDOWS):
        lanes = slice(g * POOL_GW, (g + 1) * POOL_GW)
        wsum = ext[:, POOL_HALO:POOL_HALO + tt, lanes]
        for d in range(1, w):
            wsum = wsum + ext[:, POOL_HALO - d:POOL_HALO - d + tt, lanes]
        cnt = jnp.minimum(w, pos + 1).astype(F32)
        pooled = wsum / cnt[None] - u[:, :, lanes]
        y = _dot(pooled.reshape(bb * tt, POOL_GW).astype(BF16), w_ref[g])
        y_ref[:, lanes] = y * sc_ref[:, lanes]
    ext[:, 0:POOL_HALO, :] = ext[:, tt:tt + POOL_HALO, :]


def _pool_mix(u, past, w_pool, scale, *, row0, b, t, bb, tt, pos0, prev):
    return _seq_call(
        functools.partial(_pool_kernel, bb=bb, tt=tt, pos0=pos0), name="pool_mix",
        row0=row0, b=b, t=t, bb=bb, tt=tt, n_rows=u.shape[0],
        seq_ins=[u], seq_widths=[POOL_W],
        other_ins=[past, w_pool, scale],
        other_specs=[pl.BlockSpec((bb, POOL_HALO, POOL_W), lambda i, j: (i, 0, 0)), _full(w_pool, 2), _full(scale, 2)],
        out_widths=[POOL_W], extra_out_shapes=[], extra_out_specs=[],
        scratch=[pltpu.VMEM((bb, POOL_HALO + tt, POOL_W), F32)], prev_outs=prev)[0]


def _mlstm_kernel(q_ref, k_ref, v_ref, o_ref, g_ref, c0_ref, n0_ref, m0_ref, hng_ref, *rest, bb, l, nc, n_prev):
    y_ref, c_out, n_out, m_out, cs, ns, ms = rest[n_prev:]
    c = pl.program_id(1)

    @pl.when(c == 0)
    def _():
        cs[...] = c0_ref[...]
        ns[...] = n0_ref[...]
        ms[...] = m0_ref[...]

    row = lax.broadcasted_iota(I32, (l, l), 0)
    col = lax.broadcasted_iota(I32, (l, l), 1)
    causal = row >= col
    tril = causal.astype(F32)
    lane = lax.broadcasted_iota(I32, (l, LANES), 1)
    scale = MLSTM_DH ** -0.5
    padr = LANES - l if l < 16 else 0

    def one_chunk(rows_b, state):
        gates = g_ref[rows_b, :]
        logf = _log_sigmoid(gates)
        bcum = jnp.dot(tril, logf, precision=HIGHEST, preferred_element_type=F32)
        cols = jnp.where(lane < MLSTM_HEADS, gates, bcum)
        rows = _rows_of(cols)
        new_state = []
        for h in range(MLSTM_HEADS):
            hl = slice(h * MLSTM_DH, (h + 1) * MLSTM_DH)
            i_col = cols[:, h:h + 1]
            b_col = cols[:, MLSTM_HEADS + h:MLSTM_HEADS + h + 1]
            i_row = rows[h:h + 1, :]
            b_row = rows[MLSTM_HEADS + h:MLSTM_HEADS + h + 1, :]
            c_prev, n_prev_, m_prev = state[h]
            qh = q_ref[rows_b, hl] * scale
            kh = k_ref[rows_b, hl]
            vh = v_ref[rows_b, hl]

            dmat = jnp.where(causal, b_col - b_row + i_row, -jnp.inf)
            inter = b_col + m_prev
            m_t = jnp.maximum(inter, jnp.max(dmat, axis=-1, keepdims=True))
            qk = lax.dot_general(qh, kh, (((1,), (1,)), ((), ())), preferred_element_type=F32)
            s = qk * jnp.exp(dmat - m_t)
            w_inter = jnp.exp(inter - m_t)
            num = _dot(s, vh) + w_inter * _dot(qh, c_prev)
            den = jnp.sum(s, -1, keepdims=True) + w_inter * jnp.sum(qh * n_prev_, -1, keepdims=True)
            bound = jnp.maximum(jnp.abs(den), jnp.exp(-m_t))
            hh = num / bound

            b_end = b_col[l - 1:l, :]
            gcol = b_end - b_col + i_col
            m_new = jnp.maximum(b_end + m_prev, jnp.max(gcol, axis=0, keepdims=True))
            kw = kh * jnp.exp(gcol - m_new)
            decay = jnp.exp(b_end + m_prev - m_new)
            if padr:
                zpad = jnp.zeros((padr, MLSTM_DH), F32)
                kv = lax.dot_general(jnp.concatenate([kw, zpad], 0), jnp.concatenate([vh, zpad], 0),
                                     (((0,), (0,)), ((), ())), preferred_element_type=F32)
            else:
                kv = lax.dot_general(kw, vh, (((0,), (0,)), ((), ())), preferred_element_type=F32)
            new_state.append((decay * c_prev + kv, decay * n_prev_ + jnp.sum(kw, axis=0, keepdims=True), m_new))

            mu = jnp.mean(hh, -1, keepdims=True)
            hc = hh - mu
            var = jnp.mean(hc * hc, -1, keepdims=True)
            hn = hc * lax.rsqrt(var + LN_EPS) * hng_ref[:, hl]
            y_ref[rows_b, hl] = jax.nn.sigmoid(o_ref[rows_b, hl]) * hn
        return new_state

    def per_seq(b, carry):
        state = [(cs[b, h], ns[b, h], ms[b, h][:, 0:1]) for h in range(MLSTM_HEADS)]
        for ci in range(nc):
            state = one_chunk(pl.ds(pl.multiple_of((b * nc + ci) * l, SUBLANES), l), state)
        for h in range(MLSTM_HEADS):
            cs[b, h] = state[h][0]
            ns[b, h] = state[h][1]
            ms[b, h] = jnp.broadcast_to(state[h][2], (1, LANES))
        return carry

    lax.fori_loop(0, bb, per_seq, 0)

    @pl.when(c == pl.num_programs(1) - 1)
    def _():
        c_out[...] = cs[...]
        n_out[...] = ns[...]
        m_out[...] = ms[...]


def _mlstm(q, k, v, o, g, c0, n0, m0, hn_g, *, row0, b, t, bb, l, nc, prev):
    st_c = pl.BlockSpec((bb, MLSTM_HEADS, MLSTM_DH, MLSTM_DH), lambda i, j: (i, 0, 0, 0))
    st_v = pl.BlockSpec((bb, MLSTM_HEADS, 1, LANES), lambda i, j: (i, 0, 0, 0))
    return _seq_call(
        functools.partial(_mlstm_kernel, bb=bb, l=l, nc=nc), name="mlstm",
        row0=row0, b=b, t=t, bb=bb, tt=nc * l, n_rows=q.shape[0],
        seq_ins=[q, k, v, o, g], seq_widths=[MLSTM_W] * 4 + [LANES],
        other_ins=[c0, n0, m0, hn_g], other_specs=[st_c, st_v, st_v, _full(hn_g, 2)],
        out_widths=[MLSTM_W],
        extra_out_shapes=[jax.ShapeDtypeStruct((b, MLSTM_HEADS, MLSTM_DH, MLSTM_DH), F32),
                          jax.ShapeDtypeStruct((b, MLSTM_HEADS, 1, LANES), F32),
                          jax.ShapeDtypeStruct((b, MLSTM_HEADS, 1, LANES), F32)],
        extra_out_specs=[st_c, st_v, st_v],
        scratch=[pltpu.VMEM((bb, MLSTM_HEADS, MLSTM_DH, MLSTM_DH), F32),
                 pltpu.VMEM((bb, MLSTM_HEADS, 1, LANES), F32),
                 pltpu.VMEM((bb, MLSTM_HEADS, 1, LANES), F32)],
        prev_outs=prev)


def _mix_out_kernel(*refs, n_parts):
    parts = refs[:n_parts]
    ws = refs[n_parts:2 * n_parts]
    (x_ref, bo_ref, g_ref, b_ref, wr_ref, br_ref,
     x1_ref, x1t_ref, route_ref, gate_ref, cnt_ref, carry) = refs[2 * n_parts:]
    i = pl.program_id(0)

    @pl.when(i == 0)
    def _():
        carry[...] = jnp.zeros_like(carry)

    mix = bo_ref[...]
    for a_ref, w_ref in zip(parts, ws):
        mix = mix + _dot(a_ref[...].astype(BF16), w_ref[...])
    x1 = _layer_norm(DN_ALPHA * x_ref[...] + mix, g_ref[...], b_ref[...])
    x1_ref[...] = x1
    tm = x1.shape[0]
    _store_token_tiles(x1t_ref, x1)

    logits = _dot(x1.astype(BF16), wr_ref[...]) + br_ref[...]
    lane = lax.broadcasted_iota(I32, (tm, LANES), 1)
    work = logits
    picks, vals = [], []
    onehot = jnp.zeros((tm, LANES), F32)
    for _ in range(TOP_K):
        mx = jnp.max(work, axis=-1, keepdims=True)
        pick = jnp.min(jnp.where(work == mx, lane, LANES), axis=-1, keepdims=True)
        hit = lane == pick
        onehot = jnp.where(hit, 1.0, onehot)
        work = jnp.where(hit, -jnp.inf, work)
        picks.append(pick)
        vals.append(mx)
    exps = [jnp.exp(v - vals[0]) for v in vals]
    tot = exps[0] + exps[1] + exps[2] + exps[3]

    rr = lax.broadcasted_iota(I32, (tm, tm), 0)
    cc = lax.broadcasted_iota(I32, (tm, tm), 1)
    before = _dot((rr > cc).astype(BF16), onehot.astype(BF16)) + carry[...]
    route = jnp.zeros((tm, LANES), F32)
    gate_o = jnp.zeros((tm, LANES), F32)
    for kk in range(TOP_K):
        rk = jnp.sum(jnp.where(lane == picks[kk], before, 0.0), axis=-1, keepdims=True)
        route = jnp.where(lane == kk, picks[kk].astype(F32), route)
        route = jnp.where(lane == TOP_K + kk, rk, route)
        gate_o = jnp.where(lane == kk, exps[kk] / tot, gate_o)
    route_ref[...] = _rows_of(route).astype(I32)
    gate_ref[...] = gate_o
    carry[...] = carry[...] + jnp.sum(onehot, axis=0, keepdims=True)
    cnt_ref[...] = carry[...]


def _mix_out(parts, ws, x, b_out, ln_g, ln_b, w_r, b_r):
    n = x.shape[0]
    tm = TOKEN_TILE
    row = lambda c: pl.BlockSpec((tm, c), lambda i: (i, 0))
    small = [b_out, ln_g, ln_b, w_r, b_r]
    return pl.pallas_call(
        functools.partial(_mix_out_kernel, n_parts=len(parts)),
        grid=(n // tm,),
        in_specs=[row(p.shape[1]) for p in parts] + [_full(w, 1) for w in ws] + [row(D_MODEL)]
        + [_full(a, 1) for a in small],
        out_specs=[row(D_MODEL), pl.BlockSpec((tm * TILE_ROWS, LANES), lambda i: (i, 0)),
                   pl.BlockSpec((SUBLANES, tm), lambda i: (0, i)), row(LANES),
                   pl.BlockSpec((1, LANES), lambda i: (0, 0))],
        out_shape=[
            jax.ShapeDtypeStruct((n, D_MODEL), F32),
            jax.ShapeDtypeStruct((n * TILE_ROWS, LANES), F32),
            jax.ShapeDtypeStruct((SUBLANES, n), I32),
            jax.ShapeDtypeStruct((n, LANES), F32),
            jax.ShapeDtypeStruct((1, LANES), F32),
        ],
        scratch_shapes=[pltpu.VMEM((1, LANES), F32)],
        compiler_params=_cparams(("arbitrary",)),
        name="mix_out_router",
    )(*parts, *ws, x, *small)


def _moe_kernel(w_tile, w_exp, w_lo, w_hi, w_first, w_last, w_newexp, w_valid,
                tab_hbm, x_hbm, wgu_ref, bgu_ref, wdn_ref, bdn_ref, yk_hbm,
                wgu_b, wdn_b, xbuf, ybuf, tab, tsem, gsem, ssem, *, n_tiles, tm):
    j = pl.program_id(0)
    t = w_tile[j]
    b2 = lax.rem(t, 2)

    unroll = 8

    def tab_copy(tile):
        b3 = lax.rem(tile, 3)
        return pltpu.make_async_copy(tab_hbm.at[pl.ds(tile * (2 * tm), 2 * tm)],
                                     tab.at[pl.ds(b3 * (2 * tm), 2 * tm)], tsem.at[b3])

    def token(ref, tok):
        return ref.at[pl.ds(pl.multiple_of(tok * TILE_ROWS, TILE_ROWS), TILE_ROWS), :]

    def start_gather(tile):
        base = lax.rem(tile, 3) * (2 * tm)
        buf = xbuf.at[lax.rem(tile, 2)]
        sem = gsem.at[lax.rem(tile, 2)]

        def body(i, carry):
            for u in range(unroll):
                r = i * unroll + u
                pltpu.make_async_copy(token(x_hbm, tab[base + r]), token(buf, r), sem).start()
            return carry

        lax.fori_loop(0, tm // unroll, body, 0)

    def start_scatter(tile):
        base = lax.rem(tile, 3) * (2 * tm) + tm
        buf = ybuf.at[lax.rem(tile, 2)]
        sem = ssem.at[lax.rem(tile, 2)]

        def body(i, carry):
            for u in range(unroll):
                r = i * unroll + u
                pltpu.make_async_copy(token(buf, r), token(yk_hbm, tab[base + r]), sem).start()
            return carry

        lax.fori_loop(0, tm // unroll, body, 0)

    def wait_gather(tile):
        bx = lax.rem(tile, 2)
        pltpu.make_async_copy(x_hbm.at[pl.ds(0, tm * TILE_ROWS), :], xbuf.at[bx], gsem.at[bx]).wait()

    def wait_scatter(tile):
        by = lax.rem(tile, 2)
        pltpu.make_async_copy(ybuf.at[by], yk_hbm.at[pl.ds(0, tm * TILE_ROWS), :], ssem.at[by]).wait()

    @pl.when(j == 0)
    def _():
        first_tab = tab_copy(0)
        first_tab.start()
        first_tab.wait()
        start_gather(0)
        tab_copy(1).start()

    @pl.when(w_first[j] == 1)
    def _():
        wait_gather(t)

        @pl.when(t + 1 < n_tiles)
        def _():
            tab_copy(t + 1).wait()
            start_gather(t + 1)

        @pl.when(t + 2 < n_tiles)
        def _():
            tab_copy(t + 2).start()

        @pl.when(t >= 2)
        def _():
            wait_scatter(t - 2)

        ybuf[b2] = jnp.zeros((tm * TILE_ROWS, LANES), F32)

    @pl.when(w_newexp[j] == 1)
    def _():
        wgu_b[...] = wgu_ref[0, 0].astype(BF16)
        wdn_b[...] = wdn_ref[0, 0].astype(BF16)

    @pl.when(w_valid[j] == 1)
    def _():
        gu = _dot(_load_token_tiles(xbuf.at[b2], tm, BF16), wgu_b[...]) + bgu_ref[0, 0]
        g = jnp.minimum(gu[:, :D_FF], SWIGLU_LIMIT)
        u = jnp.clip(gu[:, D_FF:], -SWIGLU_LIMIT, SWIGLU_LIMIT)
        h = (u + 1.0) * g * jax.nn.sigmoid(SWIGLU_ALPHA * g)
        y = _dot(h.astype(BF16), wdn_b[...]) + bdn_ref[0, 0]
        r = lax.broadcasted_iota(I32, (tm, 1), 0)
        mine = (r >= w_lo[j]) & (r < w_hi[j])
        _store_token_tiles(ybuf.at[b2], jnp.where(mine, y, _load_token_tiles(ybuf.at[b2], tm)))

    @pl.when(w_last[j] == 1)
    def _():
        start_scatter(t)

        @pl.when(t == n_tiles - 1)
        def _():
            wait_scatter(t - 1)
            wait_scatter(t)


def _moe_experts(l, work, tab, x1t, w_gu, b_gu, w_dn, b_dn):
    n = x1t.shape[0] // TILE_ROWS
    tm = MOE_TILE
    n_tiles = n * TOP_K // tm
    n_work = work[0].shape[0]
    by_expert = lambda j, wt, we, *_: (l, we[j], 0, 0)
    grid_spec = pltpu.PrefetchScalarGridSpec(
        num_scalar_prefetch=8,
        grid=(n_work,),
        in_specs=[
            _ANY, _ANY,
            pl.BlockSpec((1, 1, D_MODEL, 2 * D_FF), by_expert),
            pl.BlockSpec((1, 1, 1, 2 * D_FF), by_expert),
            pl.BlockSpec((1, 1, D_FF, D_MODEL), by_expert),
            pl.BlockSpec((1, 1, 1, D_MODEL), by_expert),
        ],
        out_specs=_ANY,
        scratch_shapes=[
            pltpu.VMEM((D_MODEL, 2 * D_FF), BF16), pltpu.VMEM((D_FF, D_MODEL), BF16),
            pltpu.VMEM((2, tm * TILE_ROWS, LANES), F32),
            pltpu.VMEM((2, tm * TILE_ROWS, LANES), F32),
            pltpu.SMEM((3 * 2 * tm,), I32),
            pltpu.SemaphoreType.DMA((3,)), pltpu.SemaphoreType.DMA((2,)), pltpu.SemaphoreType.DMA((2,)),
        ],
    )
    return pl.pallas_call(
        functools.partial(_moe_kernel, n_tiles=n_tiles, tm=tm),
        grid_spec=grid_spec,
        out_shape=jax.ShapeDtypeStruct((TOP_K * n * TILE_ROWS, LANES), F32),
        compiler_params=_cparams(("arbitrary",)),
        name="moe_experts",
    )(*work, tab, x1t, w_gu, b_gu[:, :, None, :], w_dn, b_dn[:, :, None, :])


def _moe_work_items(counts, n_slots):
    tm = MOE_TILE
    n_tiles = n_slots // tm
    n_work = n_tiles + N_EXPERTS
    ends = jnp.cumsum(counts)
    offs = ends - counts
    cand = jnp.concatenate([jnp.arange(n_tiles, dtype=I32) * tm, offs])
    ci = jnp.arange(n_work, dtype=I32)
    less = (cand[None, :] < cand[:, None]) | ((cand[None, :] == cand[:, None]) & (ci[None, :] < ci[:, None]))
    order = jnp.sum(less.astype(I32), axis=1)
    starts = jnp.sum(jnp.where(order[None, :] == ci[:, None], cand[None, :], 0), axis=1)
    stops = jnp.concatenate([starts[1:], jnp.array([n_slots], I32)])
    valid = stops > starts
    vpos = jnp.cumsum(valid.astype(I32)) - 1
    n_valid = jnp.sum(valid.astype(I32))
    want = jnp.minimum(ci, n_valid - 1)
    pick = valid[None, :] & (vpos[None, :] == want[:, None])
    starts = jnp.sum(jnp.where(pick, starts[None, :], 0), axis=1)
    stops = jnp.sum(jnp.where(pick, stops[None, :], 0), axis=1)
    is_valid = ci < n_valid
    tile = starts // tm
    expert = jnp.minimum(jnp.sum((ends[None, :] <= starts[:, None]).astype(I32), axis=1), N_EXPERTS - 1)
    prev = lambda a: jnp.concatenate([jnp.array([-1], I32), a[:-1]])
    nxt = lambda a: jnp.concatenate([a[1:], jnp.array([-1], I32)])
    first = (tile != prev(tile)) & is_valid
    last = ((tile != nxt(tile)) | (ci == n_valid - 1)) & is_valid
    newexp = (expert != prev(expert)) & is_valid
    as_i = lambda a: a.astype(I32)
    return offs, (tile, expert, starts - tile * tm, stops - tile * tm, as_i(first), as_i(last), as_i(newexp),
                  as_i(is_valid))


def _ffn_ln_kernel(x_ref, yk_ref, gate_ref, g_ref, b_ref, *out_refs, split_tile):
    tm = x_ref.shape[0]
    gate = gate_ref[...]
    ff = gate[:, 0:1] * _load_token_tiles(yk_ref.at[0], tm)
    for kk in range(1, TOP_K):
        ff = ff + gate[:, kk:kk + 1] * _load_token_tiles(yk_ref.at[kk], tm)
    out = _layer_norm(DN_ALPHA * x_ref[...] + ff, g_ref[...], b_ref[...])
    if split_tile is None:
        out_refs[0][...] = out
    else:
        @pl.when(pl.program_id(0) < split_tile)
        def _():
            out_refs[0][...] = out

        @pl.when(pl.program_id(0) >= split_tile)
        def _():
            out_refs[1][...] = out


def _ffn_ln(x, yk, gate, ln_g, ln_b, split=None):
    n = x.shape[0]
    tm = TOKEN_TILE
    row = lambda c: pl.BlockSpec((tm, c), lambda i: (i, 0))
    vec = pl.BlockSpec((1, D_MODEL), lambda i: (0, 0))
    if split is None:
        split_tile = None
        out_specs = row(D_MODEL)
        out_shape = jax.ShapeDtypeStruct((n, D_MODEL), F32)
    else:
        split_tile = split // tm
        out_specs = [pl.BlockSpec((tm, D_MODEL), lambda i: (jnp.minimum(i, split_tile - 1), 0)),
                     pl.BlockSpec((tm, D_MODEL), lambda i: (jnp.maximum(i - split_tile, 0), 0))]
        out_shape = [jax.ShapeDtypeStruct((split, D_MODEL), F32), jax.ShapeDtypeStruct((n - split, D_MODEL), F32)]
    return pl.pallas_call(
        functools.partial(_ffn_ln_kernel, split_tile=split_tile),
        grid=(n // tm,),
        in_specs=[row(D_MODEL), pl.BlockSpec((TOP_K, tm * TILE_ROWS, LANES), lambda i: (0, i, 0)), row(LANES),
                  vec, vec],
        out_specs=out_specs,
        out_shape=out_shape,
        compiler_params=_cparams(("arbitrary",)),
        name="ffn_ln",
    )(x, yk, gate, ln_g, ln_b)


def _moe_block(l, x1, x1t, route_t, gate, counts, ln_g, ln_b, w_gu, b_gu, w_dn, b_dn, split=None):
    n = x1.shape[0]
    n_slots = n * TOP_K
    tm = MOE_TILE
    offs, work = _moe_work_items(counts, n_slots)
    idx_t, rank_t = route_t[:TOP_K], route_t[TOP_K:]
    experts = jnp.arange(N_EXPERTS, dtype=I32)
    slot_t = rank_t + jnp.sum(jnp.where(idx_t[..., None] == experts, offs, 0), axis=-1)
    pair = jnp.argsort(slot_t.reshape(-1)).astype(I32)
    tab = jnp.concatenate([(pair % n).reshape(-1, tm), pair.reshape(-1, tm)], axis=1).reshape(-1)
    yk = _moe_experts(l, work, tab, x1t, w_gu, b_gu, w_dn, b_dn)
    return _ffn_ln(x1, yk.reshape(TOP_K, n * TILE_ROWS, LANES), gate, ln_g[l][None], ln_b[l][None], split)


def _cv_in_kernel(x_ref, w_ref, b_ref, g_ref):
    xb = x_ref[...].astype(BF16)
    a = _dot(xb, w_ref[:, :CONV_W]) + b_ref[:, :CONV_W]
    gate = _dot(xb, w_ref[:, CONV_W:]) + b_ref[:, CONV_W:]
    g_ref[...] = a * jax.nn.sigmoid(gate)


def _cv_in_proj(x, w, b):
    n = x.shape[0]
    tm = TOKEN_TILE
    row = lambda c: pl.BlockSpec((tm, c), lambda i: (i, 0))
    return pl.pallas_call(
        _cv_in_kernel,
        grid=(n // tm,),
        in_specs=[row(D_MODEL), _full(w, 1), _full(b, 1)],
        out_specs=row(CONV_W),
        out_shape=jax.ShapeDtypeStruct((n, CONV_W), F32),
        compiler_params=_cparams(("parallel",)),
        name="cv_in_proj",
    )(x, w, b)


def _dwconv_kernel(g_ref, past_ref, w_ref, wb_ref, lg_ref, lb_ref, *rest, bb, tt, n_prev):
    c_ref, ext = rest[n_prev:]
    j = pl.program_id(1)

    @pl.when(j == 0)
    def _():
        ext[:, 0:CONV_HALO, :] = past_ref[...]

    ext[:, CONV_HALO:CONV_HALO + tt, :] = g_ref[...].reshape(bb, tt, CONV_W)
    base = CONV_HALO - CONV_STATE

    def per_lane_tile(ci, carry):
        lanes = pl.ds(pl.multiple_of(ci * LANES, LANES), LANES)
        acc = ext[:, base:base + tt, lanes] * w_ref[0:1, lanes][None]
        for d in range(1, CONV_K):
            acc = acc + ext[:, base + d:base + d + tt, lanes] * w_ref[d:d + 1, lanes][None]
        c_ref[:, lanes] = acc.reshape(bb * tt, LANES) + wb_ref[:, lanes]
        return carry

    lax.fori_loop(0, CONV_W // LANES, per_lane_tile, 0)
    c = _layer_norm(c_ref[...], lg_ref[...], lb_ref[...])
    c_ref[...] = c * jax.nn.sigmoid(c)
    ext[:, 0:CONV_HALO, :] = ext[:, tt:tt + CONV_HALO, :]


def _dwconv(g, past, dw_w, dw_b, ln_g, ln_b, *, row0, b, t, bb, tt, prev):
    vec = pl.BlockSpec((1, CONV_W), lambda i, j: (0, 0))
    return _seq_call(
        functools.partial(_dwconv_kernel, bb=bb, tt=tt), name="dwconv_ln_swish",
        row0=row0, b=b, t=t, bb=bb, tt=tt, n_rows=g.shape[0],
        seq_ins=[g], seq_widths=[CONV_W],
        other_ins=[past, dw_w, dw_b, ln_g, ln_b],
        other_specs=[pl.BlockSpec((bb, CONV_HALO, CONV_W), lambda i, j: (i, 0, 0)), _full(dw_w, 2), vec, vec, vec],
        out_widths=[CONV_W], extra_out_shapes=[], extra_out_specs=[],
        scratch=[pltpu.VMEM((bb, CONV_HALO + tt, CONV_W), F32)], prev_outs=prev)[0]


def _route_args(w_router, b_router, l):
    w_r = jnp.pad(w_router[l], ((0, 0), (0, LANES - N_EXPERTS))).astype(BF16)
    b_r = jnp.pad(b_router[l], (0, LANES - N_EXPERTS), constant_values=NEG_BIG)[None]
    return w_r, b_r


def kernel(x_prompt, x_sample, state_pool, state_mlstm_C, state_mlstm_n, state_mlstm_m, state_conv, w_in_pm, b_if, w_pool, pool_scale, hn_g, w_out_pm, b_out_pm, w_in_cv, b_in_cv, dw_w, dw_b, cln_g, cln_b, w_out_cv, b_out_cv, ln_mix_g, ln_mix_b, ln_ffn_g, ln_ffn_b, w_router, b_router, w_gu, b_gu, w_dn, b_dn):
    bp, tp, _ = x_prompt.shape
    bs, ts, _ = x_sample.shape
    n_p = bp * tp
    x0 = jnp.concatenate([x_prompt.reshape(n_p, D_MODEL), x_sample.reshape(bs * ts, D_MODEL)], axis=0)

    groups = (
        dict(row0=0, b=bp, t=tp, pos0=0, pool_bb=1, pool_tt=min(512, tp), l=MLSTM_CHUNK, mlstm_bb=1,
             mlstm_nc=min(4, tp // MLSTM_CHUNK), conv_bb=1,
             conv_tt=min(256, tp),
             past_pool=jnp.zeros((bp, POOL_STATE, POOL_W), F32),
             c0=jnp.zeros((bp, MLSTM_HEADS, MLSTM_DH, MLSTM_DH), F32),
             n0=jnp.zeros((bp, MLSTM_HEADS, MLSTM_DH), F32),
             m0=jnp.zeros((bp, MLSTM_HEADS), F32),
             past_conv=jnp.zeros((bp, CONV_STATE, CONV_W), F32)),
        dict(row0=n_p, b=bs, t=ts, pos0=PAST_LEN, pool_bb=16, pool_tt=ts, l=ts, mlstm_bb=8, mlstm_nc=1, conv_bb=8,
             conv_tt=ts,
             past_pool=state_pool[0], c0=state_mlstm_C[0], n0=state_mlstm_n[0], m0=state_mlstm_m[0],
             past_conv=state_conv[0]),
    )

    def tail_state(past, a, gr, keep):
        b, t = gr["b"], gr["t"]
        new = a[gr["row0"]:gr["row0"] + b * t].reshape(b, t, a.shape[-1])
        if t >= keep:
            return new[:, t - keep:]
        return jnp.concatenate([past[:, t:], new], axis=1)

    w_in = w_in_pm[0]
    w_main = w_in[:, :POOL_W + 4 * MLSTM_W].astype(BF16)
    n_gate = 2 * MLSTM_HEADS
    w_gate = jnp.pad(w_in[:, POOL_W + 4 * MLSTM_W:], ((0, 0), (0, LANES - n_gate))).astype(BF16)
    b_gate = jnp.pad(b_if[0], (0, LANES - n_gate))[None]
    w_out = w_out_pm[0].astype(BF16)
    w_r, b_r = _route_args(w_router, b_router, 0)

    u, q, k, v, o, g = _pm_in_proj(x0, w_main, w_gate, b_gate)
    y_pool = jnp.zeros((x0.shape[0], POOL_W), F32)
    y_m = jnp.zeros((x0.shape[0], MLSTM_W), F32)
    states = []
    for gr in groups:
        b, t = gr["b"], gr["t"]
        past = jnp.pad(gr["past_pool"], ((0, 0), (POOL_HALO - POOL_STATE, 0), (0, 0)))
        y_pool = _pool_mix(u, past, w_pool[0].astype(BF16), pool_scale[0][None], row0=gr["row0"], b=b, t=t,
                           bb=gr["pool_bb"], tt=gr["pool_tt"], pos0=gr["pos0"],
                           prev=[y_pool])
        n0 = gr["n0"][:, :, None, :]
        m0 = jnp.broadcast_to(gr["m0"][:, :, None, None], (b, MLSTM_HEADS, 1, LANES))
        y_m, c_new, n_new, m_new = _mlstm(q, k, v, o, g, gr["c0"], n0, m0, hn_g[0][None], row0=gr["row0"], b=b, t=t,
                                          bb=gr["mlstm_bb"], l=gr["l"], nc=gr["mlstm_nc"], prev=[y_m])
        states.append((tail_state(gr["past_pool"], u, gr, POOL_STATE), c_new, n_new[:, :, 0, :], m_new[:, :, 0, 0]))
    x1, x1t, route_t, gate, counts = _mix_out([y_pool, y_m], [w_out[:POOL_W], w_out[POOL_W:]], x0,
                                              b_out_pm[0][None], ln_mix_g[0][None], ln_mix_b[0][None], w_r, b_r)
    x2 = _moe_block(0, x1, x1t, route_t, gate, counts[0, :N_EXPERTS].astype(I32), ln_ffn_g, ln_ffn_b,
                    w_gu, b_gu, w_dn, b_dn)

    w_r, b_r = _route_args(w_router, b_router, 1)
    dw_w_p = jnp.pad(dw_w[0], ((0, CONV_HALO - CONV_K), (0, 0)))
    gl = _cv_in_proj(x2, w_in_cv[0].astype(BF16), b_in_cv[0][None])
    c, conv_states = jnp.zeros((x0.shape[0], CONV_W), F32), []
    for gr in groups:
        past = jnp.pad(gr["past_conv"], ((0, 0), (CONV_HALO - CONV_STATE, 0), (0, 0)))
        c = _dwconv(gl, past, dw_w_p, dw_b[0][None], cln_g[0][None], cln_b[0][None], row0=gr["row0"], b=gr["b"],
                    t=gr["t"], bb=gr["conv_bb"], tt=gr["conv_tt"], prev=[c])
        conv_states.append(tail_state(gr["past_conv"], gl, gr, CONV_STATE))
    x3, x3t, route_t, gate, counts = _mix_out([c], [w_out_cv[0].astype(BF16)], x2, b_out_cv[0][None],
                                              ln_mix_g[1][None], ln_mix_b[1][None], w_r, b_r)
    y_p, y_s = _moe_block(1, x3, x3t, route_t, gate, counts[0, :N_EXPERTS].astype(I32), ln_ffn_g, ln_ffn_b,
                          w_gu, b_gu, w_dn, b_dn, split=n_p)

    (pool_p, c_p, n_p_, m_p), (pool_s, c_s, n_s, m_s) = states
    return (y_p.reshape(bp, tp, D_MODEL), y_s.reshape(bs, ts, D_MODEL),
            pool_p[None], pool_s[None], c_p[None], c_s[None], n_p_[None], n_s[None], m_p[None], m_s[None],
            conv_states[0][None], conv_states[1][None])
```

```python
import functools

import jax
import jax.numpy as jnp
from jax import lax
from jax.experimental import pallas as pl
from jax.experimental.pallas import tpu as pltpu

F32 = jnp.float32
BF16 = jnp.bfloat16
I32 = jnp.int32

D_MODEL = 1024
DEPTH = 2
POOL_W = 512
POOL_WINDOWS = (2, 4, 8, 16)
POOL_GW = 128
POOL_STATE = 15
POOL_HALO = 16
MLSTM_HEADS = 4
MLSTM_W = 512
MLSTM_DH = 128
MLSTM_CHUNK = 64
CONV_W = 1024
CONV_K = 31
CONV_STATE = 30
CONV_HALO = 32
N_EXPERTS = 32
TOP_K = 4
D_FF = 1024
SWIGLU_LIMIT = 7.0
SWIGLU_ALPHA = 1.702
LN_EPS = 1e-5
DN_ALPHA = (2 * DEPTH) ** 0.25
PAST_LEN = 16384

LANES = 128
SUBLANES = 8
TOKEN_TILE = 512
MOE_TILE = 512
NEG_BIG = -1e30
VMEM_LIMIT = 56 * 1024 * 1024
HIGHEST = lax.Precision.HIGHEST


def _cparams(sem, vmem=VMEM_LIMIT):
    return pltpu.CompilerParams(dimension_semantics=sem, vmem_limit_bytes=vmem)


def _layer_norm(x, g, b):
    mu = jnp.mean(x, -1, keepdims=True)
    xc = x - mu
    var = jnp.mean(xc * xc, -1, keepdims=True)
    return xc * lax.rsqrt(var + LN_EPS) * g + b


def _log_sigmoid(x):
    return jnp.minimum(x, 0.0) - jnp.log1p(jnp.exp(-jnp.abs(x)))


def _dot(a, b):
    return jnp.dot(a, b, preferred_element_type=F32)


def _rows_of(cols):
    sel = (lax.broadcasted_iota(I32, (SUBLANES, LANES), 0) == lax.broadcasted_iota(I32, (SUBLANES, LANES), 1))
    return lax.dot_general(sel.astype(F32), cols, (((1,), (1,)), ((), ())),
                           precision=HIGHEST, preferred_element_type=F32)


TILE_ROWS = D_MODEL // LANES


def _store_token_tiles(ref, val):
    tokens = val.shape[0]
    for s in range(TILE_ROWS):
        ref[pl.ds(s, tokens, stride=TILE_ROWS), :] = val[:, s * LANES:(s + 1) * LANES]


def _load_token_tiles(ref, tokens, dtype=F32):
    return jnp.concatenate([ref[pl.ds(s, tokens, stride=TILE_ROWS), :].astype(dtype) for s in range(TILE_ROWS)],
                           axis=1)


_ANY = pl.BlockSpec(memory_space=pl.ANY)


def _full(a, grid_rank):
    return pl.BlockSpec(a.shape, lambda *_: (0,) * a.ndim)


def _pm_in_kernel(x_ref, w_ref, wg_ref, bg_ref, u_ref, q_ref, k_ref, v_ref, o_ref, g_ref):
    xb = x_ref[...].astype(BF16)
    for j, out in enumerate((u_ref, q_ref, k_ref, v_ref, o_ref)):
        out[...] = _dot(xb, w_ref[:, j * 512:(j + 1) * 512])
    g_ref[...] = _dot(xb, wg_ref[...]) + bg_ref[...]


def _pm_in_proj(x, w_main, w_gate, b_gate):
    n = x.shape[0]
    tm = TOKEN_TILE
    row = lambda c: pl.BlockSpec((tm, c), lambda i: (i, 0))
    return pl.pallas_call(
        _pm_in_kernel,
        grid=(n // tm,),
        in_specs=[row(D_MODEL), _full(w_main, 1), _full(w_gate, 1), _full(b_gate, 1)],
        out_specs=[row(512)] * 5 + [row(LANES)],
        out_shape=[jax.ShapeDtypeStruct((n, 512), F32)] * 5 + [jax.ShapeDtypeStruct((n, LANES), F32)],
        compiler_params=_cparams(("parallel",)),
        name="pm_in_proj",
    )(x, w_main, w_gate, b_gate)


def _seq_call(kernel, *, name, row0, b, t, bb, tt, n_rows, seq_ins, seq_widths, other_ins, other_specs,
              out_widths, extra_out_shapes, extra_out_specs, scratch, prev_outs):
    steps = t // tt
    blk0 = row0 // (bb * tt)

    def seq_spec(c):
        shape = (bb * tt * TILE_ROWS, LANES) if c is None else (bb * tt, c)
        return pl.BlockSpec(shape, lambda i, j: (blk0 + i * steps + j, 0))

    prev = list(prev_outs)
    first_prev = len(seq_ins) + len(other_ins)
    aliases = {first_prev + o: o for o in range(len(out_widths))}
    return pl.pallas_call(
        functools.partial(kernel, n_prev=len(prev)),
        grid=(b // bb, steps),
        in_specs=[seq_spec(c) for c in seq_widths] + list(other_specs) + [_ANY] * len(prev),
        out_specs=[seq_spec(c) for c in out_widths] + list(extra_out_specs),
        out_shape=[jax.ShapeDtypeStruct((n_rows, c), F32) for c in out_widths] + list(extra_out_shapes),
        input_output_aliases=aliases,
        scratch_shapes=scratch,
        compiler_params=_cparams(("parallel", "arbitrary")),
        name=name,
    )(*seq_ins, *other_ins, *prev)


def _pool_kernel(u_ref, past_ref, w_ref, sc_ref, *rest, bb, tt, pos0, n_prev):
    y_ref, ext = rest[n_prev:]
    j = pl.program_id(1)

    @pl.when(j == 0)
    def _():
        ext[:, 0:POOL_HALO, :] = past_ref[...]

    u = u_ref[...].reshape(bb, tt, POOL_W)
    ext[:, POOL_HALO:POOL_HALO + tt, :] = u
    pos = pos0 + j * tt + lax.broadcasted_iota(I32, (tt, 1), 0)
    for g, w in enumerate(POOL_WINDOWS):
        lanes = slice(g * POOL_GW, (g + 1) * POOL_GW)
        wsum = ext[:, POOL_HALO:POOL_HALO + tt, lanes]
        for d in range(1, w):
            wsum = wsum + ext[:, POOL_HALO - d:POOL_HALO - d + tt, lanes]
        cnt = jnp.minimum(w, pos + 1).astype(F32)
        pooled = wsum / cnt[None] - u[:, :, lanes]
        y = _dot(pooled.reshape(bb * tt, POOL_GW).astype(BF16), w_ref[g])
        y_ref[:, lanes] = y * sc_ref[:, lanes]
    ext[:, 0:POOL_HALO, :] = ext[:, tt:tt + POOL_HALO, :]


def _pool_mix(u, past, w_pool, scale, *, row0, b, t, bb, tt, pos0, prev):
    return _seq_call(
        functools.partial(_pool_kernel, bb=bb, tt=tt, pos0=pos0), name="pool_mix",
        row0=row0, b=b, t=t, bb=bb, tt=tt, n_rows=u.shape[0],
        seq_ins=[u], seq_widths=[POOL_W],
        other_ins=[past, w_pool, scale],
        other_specs=[pl.BlockSpec((bb, POOL_HALO, POOL_W), lambda i, j: (i, 0, 0)), _full(w_pool, 2), _full(scale, 2)],
        out_widths=[POOL_W], extra_out_shapes=[], extra_out_specs=[],
        scratch=[pltpu.VMEM((bb, POOL_HALO + tt, POOL_W), F32)], prev_outs=prev)[0]


def _mlstm_kernel(q_ref, k_ref, v_ref, o_ref, g_ref, c0_ref, n0_ref, m0_ref, hng_ref, *rest, bb, l, nc, n_prev):
    y_ref, c_out, n_out, m_out, cs, ns, ms = rest[n_prev:]
    c = pl.program_id(1)

    @pl.when(c == 0)
    def _():
        cs[...] = c0_ref[...]
        ns[...] = n0_ref[...]
        ms[...] = m0_ref[...]

    row = lax.broadcasted_iota(I32, (l, l), 0)
    col = lax.broadcasted_iota(I32, (l, l), 1)
    causal = row >= col
    tril = causal.astype(F32)
    lane = lax.broadcasted_iota(I32, (l, LANES), 1)
    scale = MLSTM_DH ** -0.5
    padr = LANES - l if l < 16 else 0

    def one_chunk(rows_b, state):
        gates = g_ref[rows_b, :]
        logf = _log_sigmoid(gates)
        bcum = jnp.dot(tril, logf, precision=HIGHEST, preferred_element_type=F32)
        cols = jnp.where(lane < MLSTM_HEADS, gates, bcum)
        rows = _rows_of(cols)
        new_state = []
        for h in range(MLSTM_HEADS):
            hl = slice(h * MLSTM_DH, (h + 1) * MLSTM_DH)
            i_col = cols[:, h:h + 1]
            b_col = cols[:, MLSTM_HEADS + h:MLSTM_HEADS + h + 1]
            i_row = rows[h:h + 1, :]
            b_row = rows[MLSTM_HEADS + h:MLSTM_HEADS + h + 1, :]
            c_prev, n_prev_, m_prev = state[h]
            qh = q_ref[rows_b, hl] * scale
            kh = k_ref[rows_b, hl]
            vh = v_ref[rows_b, hl]

            dmat = jnp.where(causal, b_col - b_row + i_row, -jnp.inf)
            inter = b_col + m_prev
            m_t = jnp.maximum(inter, jnp.max(dmat, axis=-1, keepdims=True))
            qk = lax.dot_general(qh, kh, (((1,), (1,)), ((), ())), preferred_element_type=F32)
            s = qk * jnp.exp(dmat - m_t)
            w_inter = jnp.exp(inter - m_t)
            num = _dot(s, vh) + w_inter * _dot(qh, c_prev)
            den = jnp.sum(s, -1, keepdims=True) + w_inter * jnp.sum(qh * n_prev_, -1, keepdims=True)
            bound = jnp.maximum(jnp.abs(den), jnp.exp(-m_t))
            hh = num / bound

            b_end = b_col[l - 1:l, :]
            gcol = b_end - b_col + i_col
            m_new = jnp.maximum(b_end + m_prev, jnp.max(gcol, axis=0, keepdims=True))
            kw = kh * jnp.exp(gcol - m_new)
            decay = jnp.exp(b_end + m_prev - m_new)
            if padr:
                zpad = jnp.zeros((padr, MLSTM_DH), F32)
                kv = lax.dot_general(jnp.concatenate([kw, zpad], 0), jnp.concatenate([vh, zpad], 0),
                                     (((0,), (0,)), ((), ())), preferred_element_type=F32)
            else:
                kv = lax.dot_general(kw, vh, (((0,), (0,)), ((), ())), preferred_element_type=F32)
            new_state.append((decay * c_prev + kv, decay * n_prev_ + jnp.sum(kw, axis=0, keepdims=True), m_new))

            mu = jnp.mean(hh, -1, keepdims=True)
            hc = hh - mu
            var = jnp.mean(hc * hc, -1, keepdims=True)
            hn = hc * lax.rsqrt(var + LN_EPS) * hng_ref[:, hl]
            y_ref[rows_b, hl] = jax.nn.sigmoid(o_ref[rows_b, hl]) * hn
        return new_state

    def per_seq(b, carry):
        state = [(cs[b, h], ns[b, h], ms[b, h][:, 0:1]) for h in range(MLSTM_HEADS)]
        for ci in range(nc):
            state = one_chunk(pl.ds(pl.multiple_of((b * nc + ci) * l, SUBLANES), l), state)
        for h in range(MLSTM_HEADS):
            cs[b, h] = state[h][0]
            ns[b, h] = state[h][1]
            ms[b, h] = jnp.broadcast_to(state[h][2], (1, LANES))
        return carry

    lax.fori_loop(0, bb, per_seq, 0)

    @pl.when(c == pl.num_programs(1) - 1)
    def _():
        c_out[...] = cs[...]
        n_out[...] = ns[...]
        m_out[...] = ms[...]


def _mlstm(q, k, v, o, g, c0, n0, m0, hn_g, *, row0, b, t, bb, l, nc, prev):
    st_c = pl.BlockSpec((bb, MLSTM_HEADS, MLSTM_DH, MLSTM_DH), lambda i, j: (i, 0, 0, 0))
    st_v = pl.BlockSpec((bb, MLSTM_HEADS, 1, LANES), lambda i, j: (i, 0, 0, 0))
    return _seq_call(
        functools.partial(_mlstm_kernel, bb=bb, l=l, nc=nc), name="mlstm",
        row0=row0, b=b, t=t, bb=bb, tt=nc * l, n_rows=q.shape[0],
        seq_ins=[q, k, v, o, g], seq_widths=[MLSTM_W] * 4 + [LANES],
        other_ins=[c0, n0, m0, hn_g], other_specs=[st_c, st_v, st_v, _full(hn_g, 2)],
        out_widths=[MLSTM_W],
        extra_out_shapes=[jax.ShapeDtypeStruct((b, MLSTM_HEADS, MLSTM_DH, MLSTM_DH), F32),
                          jax.ShapeDtypeStruct((b, MLSTM_HEADS, 1, LANES), F32),
                          jax.ShapeDtypeStruct((b, MLSTM_HEADS, 1, LANES), F32)],
        extra_out_specs=[st_c, st_v, st_v],
        scratch=[pltpu.VMEM((bb, MLSTM_HEADS, MLSTM_DH, MLSTM_DH), F32),
                 pltpu.VMEM((bb, MLSTM_HEADS, 1, LANES), F32),
                 pltpu.VMEM((bb, MLSTM_HEADS, 1, LANES), F32)],
        prev_outs=prev)


def _mix_out_kernel(*refs, n_parts):
    parts = refs[:n_parts]
    ws = refs[n_parts:2 * n_parts]
    (x_ref, bo_ref, g_ref, b_ref, wr_ref, br_ref,
     x1_ref, x1t_ref, route_ref, gate_ref, cnt_ref, carry) = refs[2 * n_parts:]
    i = pl.program_id(0)

    @pl.when(i == 0)
    def _():
        carry[...] = jnp.zeros_like(carry)

    mix = bo_ref[...]
    for a_ref, w_ref in zip(parts, ws):
        mix = mix + _dot(a_ref[...].astype(BF16), w_ref[...])
    x1 = _layer_norm(DN_ALPHA * x_ref[...] + mix, g_ref[...], b_ref[...])
    x1_ref[...] = x1
    tm = x1.shape[0]
    _store_token_tiles(x1t_ref, x1)

    logits = _dot(x1.astype(BF16), wr_ref[...]) + br_ref[...]
    lane = lax.broadcasted_iota(I32, (tm, LANES), 1)
    work = logits
    picks, vals = [], []
    onehot = jnp.zeros((tm, LANES), F32)
    for _ in range(TOP_K):
        mx = jnp.max(work, axis=-1, keepdims=True)
        pick = jnp.min(jnp.where(work == mx, lane, LANES), axis=-1, keepdims=True)
        hit = lane == pick
        onehot = jnp.where(hit, 1.0, onehot)
        work = jnp.where(hit, -jnp.inf, work)
        picks.append(pick)
        vals.append(mx)
    exps = [jnp.exp(v - vals[0]) for v in vals]
    tot = exps[0] + exps[1] + exps[2] + exps[3]

    rr = lax.broadcasted_iota(I32, (tm, tm), 0)
    cc = lax.broadcasted_iota(I32, (tm, tm), 1)
    before = _dot((rr > cc).astype(BF16), onehot.astype(BF16)) + carry[...]
    route = jnp.zeros((tm, LANES), F32)
    gate_o = jnp.zeros((tm, LANES), F32)
    for kk in range(TOP_K):
        rk = jnp.sum(jnp.where(lane == picks[kk], before, 0.0), axis=-1, keepdims=True)
        route = jnp.where(lane == kk, picks[kk].astype(F32), route)
        route = jnp.where(lane == TOP_K + kk, rk, route)
        gate_o = jnp.where(lane == kk, exps[kk] / tot, gate_o)
    route_ref[...] = _rows_of(route).astype(I32)
    gate_ref[...] = gate_o
    carry[...] = carry[...] + jnp.sum(onehot, axis=0, keepdims=True)
    cnt_ref[...] = carry[...]


def _mix_out(parts, ws, x, b_out, ln_g, ln_b, w_r, b_r):
    n = x.shape[0]
    tm = TOKEN_TILE
    row = lambda c: pl.BlockSpec((tm, c), lambda i: (i, 0))
    small = [b_out, ln_g, ln_b, w_r, b_r]
    return pl.pallas_call(
        functools.partial(_mix_out_kernel, n_parts=len(parts)),
        grid=(n // tm,),
        in_specs=[row(p.shape[1]) for p in parts] + [_full(w, 1) for w in ws] + [row(D_MODEL)]
        + [_full(a, 1) for a in small],
        out_specs=[row(D_MODEL), pl.BlockSpec((tm * TILE_ROWS, LANES), lambda i: (i, 0)),
                   pl.BlockSpec((SUBLANES, tm), lambda i: (0, i)), row(LANES),
                   pl.BlockSpec((1, LANES), lambda i: (0, 0))],
        out_shape=[
            jax.ShapeDtypeStruct((n, D_MODEL), F32),
            jax.ShapeDtypeStruct((n * TILE_ROWS, LANES), F32),
            jax.ShapeDtypeStruct((SUBLANES, n), I32),
            jax.ShapeDtypeStruct((n, LANES), F32),
            jax.ShapeDtypeStruct((1, LANES), F32),
        ],
        scratch_shapes=[pltpu.VMEM((1, LANES), F32)],
        compiler_params=_cparams(("arbitrary",)),
        name="mix_out_router",
    )(*parts, *ws, x, *small)


MOE_CHUNKS = 4
MOE_TAB_RING = 4
MOE_PAD_TILES = 2


def _moe_kernel(w_tile, w_exp, w_lo, w_hi, w_first, w_last, w_newexp, w_valid,
                tab_hbm, x_hbm, wgu_ref, bgu_ref, wdn_ref, bdn_ref, yk_hbm,
                wgu_b, wdn_b, h_scr, xbuf, ybuf, tab, tsem, gsem, ssem, *, n_tiles, tm):
    j = pl.program_id(0)
    t = w_tile[j]
    b2 = t & 1
    unroll = 8

    def tab_slot(tile):
        return (tile + MOE_PAD_TILES) & (MOE_TAB_RING - 1)

    def tab_copy(tile):
        return pltpu.make_async_copy(tab_hbm.at[pl.ds((tile + MOE_PAD_TILES) * (2 * tm), 2 * tm)],
                                     tab.at[pl.ds(tab_slot(tile) * (2 * tm), 2 * tm)], tsem.at[tab_slot(tile)])

    def token(ref, tok):
        return ref.at[pl.ds(pl.multiple_of(tok * TILE_ROWS, TILE_ROWS), TILE_ROWS), :]

    def gather_copy(tile, r):
        src = tab[tab_slot(tile) * (2 * tm) + r]
        return pltpu.make_async_copy(token(x_hbm, src), token(xbuf.at[tile & 1], r), gsem.at[tile & 1])

    def scatter_copy(tile, r):
        dst = tab[tab_slot(tile) * (2 * tm) + tm + r]
        return pltpu.make_async_copy(token(ybuf.at[tile & 1], r), token(yk_hbm, dst), ssem.at[tile & 1])

    def start_all(copy, tile):
        def body(i, carry):
            for u in range(unroll):
                copy(tile, i * unroll + u).start()
            return carry

        lax.fori_loop(0, tm // unroll, body, 0)

    def wait_gather(tile):
        pltpu.make_async_copy(x_hbm.at[pl.ds(0, tm * TILE_ROWS), :], xbuf.at[tile & 1], gsem.at[tile & 1]).wait()

    def wait_scatter(tile):
        pltpu.make_async_copy(ybuf.at[tile & 1], yk_hbm.at[pl.ds(0, tm * TILE_ROWS), :], ssem.at[tile & 1]).wait()

    @pl.when(j == 0)
    def _():
        for tile in (-2, -1, 0):
            cp = tab_copy(tile)
            cp.start()
            cp.wait()
        tab_copy(1).start()
        ybuf[...] = jnp.zeros(ybuf.shape, F32)
        start_all(scatter_copy, -2)
        start_all(gather_copy, 0)

    @pl.when(w_newexp[j] == 1)
    def _():
        wgu_b[...] = wgu_ref[0, 0].astype(BF16)
        wdn_b[...] = wdn_ref[0, 0].astype(BF16)

    chunk = D_FF // MOE_CHUNKS
    rows_per_chunk = tm // MOE_CHUNKS

    def experts(first_item):
        if first_item:
            wait_gather(t)
            tab_copy(t + 1).wait()
            tab_copy(t + 2).start()
        x = _load_token_tiles(xbuf.at[b2], tm, BF16)
        bgu = bgu_ref[0, 0]
        for c in range(MOE_CHUNKS):
            cg = slice(c * chunk, (c + 1) * chunk)
            cu = slice(D_FF + c * chunk, D_FF + (c + 1) * chunk)
            g = jnp.minimum(_dot(x, wgu_b[:, cg]) + bgu[:, cg], SWIGLU_LIMIT)
            u = jnp.clip(_dot(x, wgu_b[:, cu]) + bgu[:, cu], -SWIGLU_LIMIT, SWIGLU_LIMIT)
            h_scr[:, cg] = ((u + 1.0) * g * jax.nn.sigmoid(SWIGLU_ALPHA * g)).astype(BF16)
            if first_item:
                for r in range(c * rows_per_chunk, (c + 1) * rows_per_chunk):
                    gather_copy(t + 1, r).start()
        if first_item:
            wait_scatter(t - 2)
        else:
            row = lax.broadcasted_iota(I32, (tm, 1), 0)
            mine = (row >= w_lo[j]) & (row < w_hi[j])
        h = h_scr[...]
        bdn = bdn_ref[0, 0]
        out = ybuf.at[b2]
        for c in range(MOE_CHUNKS):
            cy = slice(c * chunk, (c + 1) * chunk)
            y = _dot(h, wdn_b[:, cy]) + bdn[:, cy]
            for s in range(chunk // LANES):
                piece = y[:, s * LANES:(s + 1) * LANES]
                rows = pl.ds(c * (chunk // LANES) + s, tm, stride=TILE_ROWS)
                out[rows, :] = piece if first_item else jnp.where(mine, piece, out[rows, :])
            if first_item:
                for r in range(c * rows_per_chunk, (c + 1) * rows_per_chunk):
                    scatter_copy(t - 1, r).start()

    @pl.when(w_first[j] == 1)
    def _():
        experts(True)

    @pl.when((w_valid[j] == 1) & (w_first[j] == 0))
    def _():
        experts(False)

    @pl.when((w_last[j] == 1) & (t == n_tiles - 1))
    def _():
        start_all(scatter_copy, t)
        wait_scatter(t - 1)
        wait_scatter(t)
        wait_gather(t + 1)
        tab_copy(t + 2).wait()


def _moe_experts(l, work, tab, x1t, w_gu, b_gu, w_dn, b_dn):
    n = x1t.shape[0] // TILE_ROWS
    tm = MOE_TILE
    n_tiles = n * TOP_K // tm
    n_work = work[0].shape[0]
    by_expert = lambda j, wt, we, *_: (l, we[j], 0, 0)
    grid_spec = pltpu.PrefetchScalarGridSpec(
        num_scalar_prefetch=8,
        grid=(n_work,),
        in_specs=[
            _ANY, _ANY,
            pl.BlockSpec((1, 1, D_MODEL, 2 * D_FF), by_expert),
            pl.BlockSpec((1, 1, 1, 2 * D_FF), by_expert),
            pl.BlockSpec((1, 1, D_FF, D_MODEL), by_expert),
            pl.BlockSpec((1, 1, 1, D_MODEL), by_expert),
        ],
        out_specs=_ANY,
        scratch_shapes=[
            pltpu.VMEM((D_MODEL, 2 * D_FF), BF16), pltpu.VMEM((D_FF, D_MODEL), BF16),
            pltpu.VMEM((tm, D_FF), BF16),
            pltpu.VMEM((2, tm * TILE_ROWS, LANES), F32),
            pltpu.VMEM((2, tm * TILE_ROWS, LANES), F32),
            pltpu.SMEM((MOE_TAB_RING * 2 * tm,), I32),
            pltpu.SemaphoreType.DMA((MOE_TAB_RING,)), pltpu.SemaphoreType.DMA((2,)), pltpu.SemaphoreType.DMA((2,)),
        ],
    )
    return pl.pallas_call(
        functools.partial(_moe_kernel, n_tiles=n_tiles, tm=tm),
        grid_spec=grid_spec,
        out_shape=jax.ShapeDtypeStruct(((TOP_K * n + MOE_PAD_TILES * tm) * TILE_ROWS, LANES), F32),
        compiler_params=_cparams(("arbitrary",)),
        name="moe_experts",
    )(*work, tab, x1t, w_gu, b_gu[:, :, None, :], w_dn, b_dn[:, :, None, :])


def _moe_work_items(counts, n_slots):
    tm = MOE_TILE
    n_tiles = n_slots // tm
    n_work = n_tiles + N_EXPERTS
    ends = jnp.cumsum(counts)
    offs = ends - counts
    cand = jnp.concatenate([jnp.arange(n_tiles, dtype=I32) * tm, offs])
    ci = jnp.arange(n_work, dtype=I32)
    less = (cand[None, :] < cand[:, None]) | ((cand[None, :] == cand[:, None]) & (ci[None, :] < ci[:, None]))
    order = jnp.sum(less.astype(I32), axis=1)
    starts = jnp.sum(jnp.where(order[None, :] == ci[:, None], cand[None, :], 0), axis=1)
    stops = jnp.concatenate([starts[1:], jnp.array([n_slots], I32)])
    valid = stops > starts
    vpos = jnp.cumsum(valid.astype(I32)) - 1
    n_valid = jnp.sum(valid.astype(I32))
    want = jnp.minimum(ci, n_valid - 1)
    pick = valid[None, :] & (vpos[None, :] == want[:, None])
    starts = jnp.sum(jnp.where(pick, starts[None, :], 0), axis=1)
    stops = jnp.sum(jnp.where(pick, stops[None, :], 0), axis=1)
    is_valid = ci < n_valid
    tile = starts // tm
    expert = jnp.minimum(jnp.sum((ends[None, :] <= starts[:, None]).astype(I32), axis=1), N_EXPERTS - 1)
    prev = lambda a: jnp.concatenate([jnp.array([-1], I32), a[:-1]])
    nxt = lambda a: jnp.concatenate([a[1:], jnp.array([-1], I32)])
    first = (tile != prev(tile)) & is_valid
    last = ((tile != nxt(tile)) | (ci == n_valid - 1)) & is_valid
    newexp = (expert != prev(expert)) & is_valid
    as_i = lambda a: a.astype(I32)
    return offs, (tile, expert, starts - tile * tm, stops - tile * tm, as_i(first), as_i(last), as_i(newexp),
                  as_i(is_valid))


def _ffn_ln_kernel(x_ref, *rest, split_tile):
    yk_refs = rest[:TOP_K]
    gate_ref, g_ref, b_ref = rest[TOP_K:TOP_K + 3]
    out_refs = rest[TOP_K + 3:]
    tm = x_ref.shape[0]
    gate = gate_ref[...]
    ff = gate[:, 0:1] * _load_token_tiles(yk_refs[0], tm)
    for kk in range(1, TOP_K):
        ff = ff + gate[:, kk:kk + 1] * _load_token_tiles(yk_refs[kk], tm)
    out = _layer_norm(DN_ALPHA * x_ref[...] + ff, g_ref[...], b_ref[...])
    if split_tile is None:
        out_refs[0][...] = out
    else:
        @pl.when(pl.program_id(0) < split_tile)
        def _():
            out_refs[0][...] = out

        @pl.when(pl.program_id(0) >= split_tile)
        def _():
            out_refs[1][...] = out


def _ffn_ln(x, yk, gate, ln_g, ln_b, split=None):
    n = x.shape[0]
    tm = TOKEN_TILE
    pick = lambda kk: pl.BlockSpec((tm * TILE_ROWS, LANES), lambda i: (kk * (n // tm) + i, 0))
    row = lambda c: pl.BlockSpec((tm, c), lambda i: (i, 0))
    vec = pl.BlockSpec((1, D_MODEL), lambda i: (0, 0))
    if split is None:
        split_tile = None
        out_specs = row(D_MODEL)
        out_shape = jax.ShapeDtypeStruct((n, D_MODEL), F32)
    else:
        split_tile = split // tm
        out_specs = [pl.BlockSpec((tm, D_MODEL), lambda i: (jnp.minimum(i, split_tile - 1), 0)),
                     pl.BlockSpec((tm, D_MODEL), lambda i: (jnp.maximum(i - split_tile, 0), 0))]
        out_shape = [jax.ShapeDtypeStruct((split, D_MODEL), F32), jax.ShapeDtypeStruct((n - split, D_MODEL), F32)]
    return pl.pallas_call(
        functools.partial(_ffn_ln_kernel, split_tile=split_tile),
        grid=(n // tm,),
        in_specs=[row(D_MODEL)] + [pick(kk) for kk in range(TOP_K)] + [row(LANES), vec, vec],
        out_specs=out_specs,
        out_shape=out_shape,
        compiler_params=_cparams(("arbitrary",)),
        name="ffn_ln",
    )(x, *([yk] * TOP_K), gate, ln_g, ln_b)


def _moe_block(l, x1, x1t, route_t, gate, counts, ln_g, ln_b, w_gu, b_gu, w_dn, b_dn, split=None):
    n = x1.shape[0]
    n_slots = n * TOP_K
    tm = MOE_TILE
    offs, work = _moe_work_items(counts, n_slots)
    idx_t, rank_t = route_t[:TOP_K], route_t[TOP_K:]
    experts = jnp.arange(N_EXPERTS, dtype=I32)
    slot_t = rank_t + jnp.sum(jnp.where(idx_t[..., None] == experts, offs, 0), axis=-1)
    pair = jnp.argsort(slot_t.reshape(-1)).astype(I32)
    spare = n_slots + jnp.arange(MOE_PAD_TILES * tm, dtype=I32).reshape(MOE_PAD_TILES, tm)
    first = jnp.arange(MOE_PAD_TILES * tm, dtype=I32).reshape(MOE_PAD_TILES, tm) % tm
    src = jnp.concatenate([first, (pair % n).reshape(-1, tm), first], axis=0)
    dst = jnp.concatenate([spare, pair.reshape(-1, tm), spare], axis=0)
    tab = jnp.concatenate([src, dst], axis=1).reshape(-1)
    yk = _moe_experts(l, work, tab, x1t, w_gu, b_gu, w_dn, b_dn)
    return _ffn_ln(x1, yk, gate, ln_g[l][None], ln_b[l][None], split)


def _cv_in_kernel(x_ref, w_ref, b_ref, g_ref):
    xb = x_ref[...].astype(BF16)
    a = _dot(xb, w_ref[:, :CONV_W]) + b_ref[:, :CONV_W]
    gate = _dot(xb, w_ref[:, CONV_W:]) + b_ref[:, CONV_W:]
    _store_token_tiles(g_ref, a * jax.nn.sigmoid(gate))


def _cv_in_proj(x, w, b):
    n = x.shape[0]
    tm = TOKEN_TILE
    return pl.pallas_call(
        _cv_in_kernel,
        grid=(n // tm,),
        in_specs=[pl.BlockSpec((tm, D_MODEL), lambda i: (i, 0)), _full(w, 1), _full(b, 1)],
        out_specs=pl.BlockSpec((tm * TILE_ROWS, LANES), lambda i: (i, 0)),
        out_shape=jax.ShapeDtypeStruct((n * TILE_ROWS, LANES), F32),
        compiler_params=_cparams(("parallel",)),
        name="cv_in_proj",
    )(x, w, b)


CONV_GROUP = 16


def _dwconv_kernel(g_ref, past_ref, w_ref, wb_ref, lg_ref, lb_ref, *rest, bb, tt, n_prev):
    c_ref, ext, conv = rest[n_prev:]
    j = pl.program_id(1)
    halo = CONV_HALO * TILE_ROWS

    @pl.when(j == 0)
    def _():
        ext[:, 0:halo, :] = past_ref[...]

    ext[:, halo:, :] = g_ref[...].reshape(bb, tt * TILE_ROWS, LANES)
    group = min(CONV_GROUP, tt)
    first_tap = CONV_HALO - CONV_STATE

    for b in range(bb):
        def per_group(gi, carry):
            t0 = gi * group
            acc = None
            for d in range(CONV_K):
                rows = pl.ds(pl.multiple_of((first_tap + d + t0) * TILE_ROWS, TILE_ROWS), group * TILE_ROWS)
                window = ext[b, rows, :].reshape(group, TILE_ROWS, LANES)
                term = window * w_ref[d * TILE_ROWS:(d + 1) * TILE_ROWS, :][None]
                acc = term if acc is None else acc + term
            out_rows = pl.ds(pl.multiple_of((b * tt + t0) * TILE_ROWS, TILE_ROWS), group * TILE_ROWS)
            conv[out_rows, :] = acc.reshape(group * TILE_ROWS, LANES)
            return carry

        lax.fori_loop(0, tt // group, per_group, 0)

    c = _layer_norm(_load_token_tiles(conv, bb * tt) + wb_ref[...], lg_ref[...], lb_ref[...])
    c_ref[...] = c * jax.nn.sigmoid(c)
    ext[:, 0:halo, :] = ext[:, tt * TILE_ROWS:tt * TILE_ROWS + halo, :]


def _dwconv(g, past, dw_w, dw_b, ln_g, ln_b, *, row0, b, t, bb, tt, prev):
    vec = pl.BlockSpec((1, CONV_W), lambda i, j: (0, 0))
    return _seq_call(
        functools.partial(_dwconv_kernel, bb=bb, tt=tt), name="dwconv_ln_swish",
        row0=row0, b=b, t=t, bb=bb, tt=tt, n_rows=g.shape[0] // TILE_ROWS,
        seq_ins=[g], seq_widths=[None],
        other_ins=[past, dw_w, dw_b, ln_g, ln_b],
        other_specs=[pl.BlockSpec((bb, CONV_HALO * TILE_ROWS, LANES), lambda i, j: (i, 0, 0)), _full(dw_w, 2),
                     vec, vec, vec],
        out_widths=[CONV_W], extra_out_shapes=[], extra_out_specs=[],
        scratch=[pltpu.VMEM((bb, (CONV_HALO + tt) * TILE_ROWS, LANES), F32),
                 pltpu.VMEM((bb * tt * TILE_ROWS, LANES), F32)],
        prev_outs=prev)[0]


def _route_args(w_router, b_router, l):
    w_r = jnp.pad(w_router[l], ((0, 0), (0, LANES - N_EXPERTS))).astype(BF16)
    b_r = jnp.pad(b_router[l], (0, LANES - N_EXPERTS), constant_values=NEG_BIG)[None]
    return w_r, b_r


def kernel(x_prompt, x_sample, state_pool, state_mlstm_C, state_mlstm_n, state_mlstm_m, state_conv, w_in_pm, b_if, w_pool, pool_scale, hn_g, w_out_pm, b_out_pm, w_in_cv, b_in_cv, dw_w, dw_b, cln_g, cln_b, w_out_cv, b_out_cv, ln_mix_g, ln_mix_b, ln_ffn_g, ln_ffn_b, w_router, b_router, w_gu, b_gu, w_dn, b_dn):
    bp, tp, _ = x_prompt.shape
    bs, ts, _ = x_sample.shape
    n_p = bp * tp
    x0 = jnp.concatenate([x_prompt.reshape(n_p, D_MODEL), x_sample.reshape(bs * ts, D_MODEL)], axis=0)

    groups = (
        dict(row0=0, b=bp, t=tp, pos0=0, pool_bb=1, pool_tt=min(512, tp), l=MLSTM_CHUNK, mlstm_bb=1,
             mlstm_nc=min(4, tp // MLSTM_CHUNK), conv_bb=1,
             conv_tt=min(256, tp),
             past_pool=jnp.zeros((bp, POOL_STATE, POOL_W), F32),
             c0=jnp.zeros((bp, MLSTM_HEADS, MLSTM_DH, MLSTM_DH), F32),
             n0=jnp.zeros((bp, MLSTM_HEADS, MLSTM_DH), F32),
             m0=jnp.zeros((bp, MLSTM_HEADS), F32),
             past_conv=jnp.zeros((bp, CONV_STATE, CONV_W), F32)),
        dict(row0=n_p, b=bs, t=ts, pos0=PAST_LEN, pool_bb=16, pool_tt=ts, l=ts, mlstm_bb=8, mlstm_nc=1, conv_bb=8,
             conv_tt=ts,
             past_pool=state_pool[0], c0=state_mlstm_C[0], n0=state_mlstm_n[0], m0=state_mlstm_m[0],
             past_conv=state_conv[0]),
    )

    def tail_state(past, a, gr, keep, tiled=False):
        b, t, row0 = gr["b"], gr["t"], gr["row0"]
        rpt = TILE_ROWS if tiled else 1
        width = a.shape[-1] * rpt
        if t >= keep:
            parts = [a[(row0 + i * t + t - keep) * rpt:(row0 + (i + 1) * t) * rpt] for i in range(b)]
            return jnp.stack(parts).reshape(b, keep, width)
        new = a[row0 * rpt:(row0 + b * t) * rpt].reshape(b, t, width)
        return jnp.concatenate([past[:, t:], new], axis=1)

    w_in = w_in_pm[0]
    w_main = w_in[:, :POOL_W + 4 * MLSTM_W].astype(BF16)
    n_gate = 2 * MLSTM_HEADS
    w_gate = jnp.pad(w_in[:, POOL_W + 4 * MLSTM_W:], ((0, 0), (0, LANES - n_gate))).astype(BF16)
    b_gate = jnp.pad(b_if[0], (0, LANES - n_gate))[None]
    w_out = w_out_pm[0].astype(BF16)
    w_r, b_r = _route_args(w_router, b_router, 0)

    u, q, k, v, o, g = _pm_in_proj(x0, w_main, w_gate, b_gate)
    y_pool = jnp.zeros((x0.shape[0], POOL_W), F32)
    y_m = jnp.zeros((x0.shape[0], MLSTM_W), F32)
    states = []
    for gr in groups:
        b, t = gr["b"], gr["t"]
        past = jnp.pad(gr["past_pool"], ((0, 0), (POOL_HALO - POOL_STATE, 0), (0, 0)))
        y_pool = _pool_mix(u, past, w_pool[0].astype(BF16), pool_scale[0][None], row0=gr["row0"], b=b, t=t,
                           bb=gr["pool_bb"], tt=gr["pool_tt"], pos0=gr["pos0"],
                           prev=[y_pool])
        n0 = gr["n0"][:, :, None, :]
        m0 = jnp.broadcast_to(gr["m0"][:, :, None, None], (b, MLSTM_HEADS, 1, LANES))
        y_m, c_new, n_new, m_new = _mlstm(q, k, v, o, g, gr["c0"], n0, m0, hn_g[0][None], row0=gr["row0"], b=b, t=t,
                                          bb=gr["mlstm_bb"], l=gr["l"], nc=gr["mlstm_nc"], prev=[y_m])
        states.append((tail_state(gr["past_pool"], u, gr, POOL_STATE), c_new, n_new[:, :, 0, :], m_new[:, :, 0, 0]))
    x1, x1t, route_t, gate, counts = _mix_out([y_pool, y_m], [w_out[:POOL_W], w_out[POOL_W:]], x0,
                                              b_out_pm[0][None], ln_mix_g[0][None], ln_mix_b[0][None], w_r, b_r)
    x2 = _moe_block(0, x1, x1t, route_t, gate, counts[0, :N_EXPERTS].astype(I32), ln_ffn_g, ln_ffn_b,
                    w_gu, b_gu, w_dn, b_dn)

    w_r, b_r = _route_args(w_router, b_router, 1)
    dw_w_p = jnp.pad(dw_w[0], ((0, CONV_HALO - CONV_K), (0, 0))).reshape(CONV_HALO * TILE_ROWS, LANES)
    gl = _cv_in_proj(x2, w_in_cv[0].astype(BF16), b_in_cv[0][None])
    c, conv_states = jnp.zeros((x0.shape[0], CONV_W), F32), []
    for gr in groups:
        past = jnp.pad(gr["past_conv"], ((0, 0), (CONV_HALO - CONV_STATE, 0), (0, 0)))
        past = past.reshape(gr["b"], CONV_HALO * TILE_ROWS, LANES)
        c = _dwconv(gl, past, dw_w_p, dw_b[0][None], cln_g[0][None], cln_b[0][None], row0=gr["row0"], b=gr["b"],
                    t=gr["t"], bb=gr["conv_bb"], tt=gr["conv_tt"], prev=[c])
        conv_states.append(tail_state(gr["past_conv"], gl, gr, CONV_STATE, tiled=True))
    x3, x3t, route_t, gate, counts = _mix_out([c], [w_out_cv[0].astype(BF16)], x2, b_out_cv[0][None],
                                              ln_mix_g[1][None], ln_mix_b[1][None], w_r, b_r)
    y_p, y_s = _moe_block(1, x3, x3t, route_t, gate, counts[0, :N_EXPERTS].astype(I32), ln_ffn_g, ln_ffn_b,
                          w_gu, b_gu, w_dn, b_dn, split=n_p)

    (pool_p, c_p, n_p_, m_p), (pool_s, c_s, n_s, m_s) = states
    return (y_p.reshape(bp, tp, D_MODEL), y_s.reshape(bs, ts, D_MODEL),
            pool_p[None], pool_s[None], c_p[None], c_s[None], n_p_[None], n_s[None], m_p[None], m_s[None],
            conv_states[0][None], conv_states[1][None])
```

```python
import functools

import jax
import jax.numpy as jnp
from jax import lax
from jax.experimental import pallas as pl
from jax.experimental.pallas import tpu as pltpu

F32 = jnp.float32
BF16 = jnp.bfloat16
I32 = jnp.int32

D_MODEL = 1024
DEPTH = 2
POOL_W = 512
POOL_WINDOWS = (2, 4, 8, 16)
POOL_GW = 128
POOL_STATE = 15
POOL_HALO = 16
MLSTM_HEADS = 4
MLSTM_W = 512
MLSTM_DH = 128
MLSTM_CHUNK = 64
MLSTM_LOCKSTEP_SEQS = 2
CONV_W = 1024
CONV_K = 31
CONV_STATE = 30
CONV_HALO = 32
N_EXPERTS = 32
TOP_K = 4
D_FF = 1024
SWIGLU_LIMIT = 7.0
SWIGLU_ALPHA = 1.702
LN_EPS = 1e-5
DN_ALPHA = (2 * DEPTH) ** 0.25
PAST_LEN = 16384

LANES = 128
SUBLANES = 8
TOKEN_TILE = 512
MOE_TILE = 512
NEG_BIG = -1e30
VMEM_LIMIT = 56 * 1024 * 1024


def _cparams(sem, vmem=VMEM_LIMIT):
    return pltpu.CompilerParams(dimension_semantics=sem, vmem_limit_bytes=vmem)


def _layer_norm(x, g, b):
    mu = jnp.mean(x, -1, keepdims=True)
    xc = x - mu
    var = jnp.mean(xc * xc, -1, keepdims=True)
    return xc * lax.rsqrt(var + LN_EPS) * g + b


def _log_sigmoid(x):
    return jnp.minimum(x, 0.0) - jnp.log1p(jnp.exp(-jnp.abs(x)))


def _dot(a, b):
    return jnp.dot(a, b, preferred_element_type=F32)


def _bf16_pieces(x):
    hi = x.astype(BF16)
    rest = x - hi.astype(F32)
    mid = rest.astype(BF16)
    return hi, mid, (rest - mid.astype(F32)).astype(BF16)


def _dot_exact_lhs(a, x, dims=(((1,), (0,)), ((), ()))):
    a = a.astype(BF16)
    return sum(lax.dot_general(a, p, dims, preferred_element_type=F32) for p in _bf16_pieces(x))


def _rows_of(cols):
    sel = (lax.broadcasted_iota(I32, (SUBLANES, LANES), 0) == lax.broadcasted_iota(I32, (SUBLANES, LANES), 1))
    return _dot_exact_lhs(sel, cols, (((1,), (1,)), ((), ())))


TILE_ROWS = D_MODEL // LANES


def _store_token_tiles(ref, val):
    tokens = val.shape[0]
    for s in range(TILE_ROWS):
        ref[pl.ds(s, tokens, stride=TILE_ROWS), :] = val[:, s * LANES:(s + 1) * LANES]


def _load_token_tiles(ref, tokens, dtype=F32):
    return jnp.concatenate([ref[pl.ds(s, tokens, stride=TILE_ROWS), :].astype(dtype) for s in range(TILE_ROWS)],
                           axis=1)


_ANY = pl.BlockSpec(memory_space=pl.ANY)


def _full(a, grid_rank):
    return pl.BlockSpec(a.shape, lambda *_: (0,) * a.ndim)


def _token_specs(x_srcs):
    tm = TOKEN_TILE
    if len(x_srcs) == 1:
        return [pl.BlockSpec((tm, D_MODEL), lambda i: (i, 0))], None, x_srcs[0].shape[0]
    x_p, x_s = x_srcs
    b, t, _ = x_p.shape
    chunk = tm // b
    tiles_p = t // chunk
    assert b * chunk == tm and tiles_p * chunk == t and x_s.shape[0] % tm == 0
    specs = [pl.BlockSpec((b, chunk, D_MODEL), lambda i: (0, jnp.minimum(i, tiles_p - 1), 0)),
             pl.BlockSpec((tm, D_MODEL), lambda i: (jnp.maximum(i - tiles_p, 0), 0))]
    return specs, tiles_p, b * t + x_s.shape[0]


def _token_rows(x_refs, tiles_p):
    if len(x_refs) == 1:
        return x_refs[0][...]
    x_p, x_s = x_refs
    return jnp.where(pl.program_id(0) < tiles_p, x_p[...].reshape(TOKEN_TILE, D_MODEL), x_s[...])


def _pm_in_kernel(*refs, n_x, tiles_p):
    x_refs = refs[:n_x]
    w_ref, wg_ref, bg_ref, u_ref, q_ref, k_ref, v_ref, o_ref, g_ref = refs[n_x:]
    xb = _token_rows(x_refs, tiles_p).astype(BF16)
    for j, out in enumerate((u_ref, q_ref, k_ref, v_ref, o_ref)):
        out[...] = _dot(xb, w_ref[:, j * 512:(j + 1) * 512])
    g_ref[...] = _dot(xb, wg_ref[...]) + bg_ref[...]


def _pm_in_proj(x_srcs, w_main, w_gate, b_gate):
    tm = TOKEN_TILE
    x_specs, tiles_p, n = _token_specs(x_srcs)
    row = lambda c: pl.BlockSpec((tm, c), lambda i: (i, 0))
    return pl.pallas_call(
        functools.partial(_pm_in_kernel, n_x=len(x_srcs), tiles_p=tiles_p),
        grid=(n // tm,),
        in_specs=x_specs + [_full(w_main, 1), _full(w_gate, 1), _full(b_gate, 1)],
        out_specs=[row(512)] * 5 + [row(LANES)],
        out_shape=[jax.ShapeDtypeStruct((n, 512), F32)] * 5 + [jax.ShapeDtypeStruct((n, LANES), F32)],
        compiler_params=_cparams(("arbitrary",)),
        name="pm_in_proj",
    )(*x_srcs, w_main, w_gate, b_gate)


def _seq_call(kernel, *, name, row0, b, t, bb, tt, n_rows, seq_ins, seq_widths, other_ins, other_specs,
              out_widths, extra_out_shapes, extra_out_specs, scratch, prev_outs):
    steps = t // tt
    blk0 = row0 // (bb * tt)

    def seq_spec(c):
        shape = (bb * tt * TILE_ROWS, LANES) if c is None else (bb * tt, c)
        return pl.BlockSpec(shape, lambda i, j: (blk0 + i * steps + j, 0))

    prev = list(prev_outs)
    first_prev = len(seq_ins) + len(other_ins)
    aliases = {first_prev + o: o for o in range(len(out_widths))}
    return pl.pallas_call(
        functools.partial(kernel, n_prev=len(prev)),
        grid=(b // bb, steps),
        in_specs=[seq_spec(c) for c in seq_widths] + list(other_specs) + [_ANY] * len(prev),
        out_specs=[seq_spec(c) for c in out_widths] + list(extra_out_specs),
        out_shape=[jax.ShapeDtypeStruct((n_rows, c), F32) for c in out_widths] + list(extra_out_shapes),
        input_output_aliases=aliases,
        scratch_shapes=scratch,
        compiler_params=_cparams(("parallel", "arbitrary")),
        name=name,
    )(*seq_ins, *other_ins, *prev)


def _pool_kernel(u_ref, past_ref, w_ref, sc_ref, *rest, bb, tt, pos0, n_prev):
    y_ref, ext = rest[n_prev:]
    j = pl.program_id(1)

    @pl.when(j == 0)
    def _():
        ext[:, 0:POOL_HALO, :] = past_ref[...]

    u = u_ref[...].reshape(bb, tt, POOL_W)
    ext[:, POOL_HALO:POOL_HALO + tt, :] = u
    pos = pos0 + j * tt + lax.broadcasted_iota(I32, (tt, 1), 0)
    for g, w in enumerate(POOL_WINDOWS):
        lanes = slice(g * POOL_GW, (g + 1) * POOL_GW)
        wsum = ext[:, POOL_HALO:POOL_HALO + tt, lanes]
        for d in range(1, w):
            wsum = wsum + ext[:, POOL_HALO - d:POOL_HALO - d + tt, lanes]
        cnt = jnp.minimum(w, pos + 1).astype(F32)
        pooled = wsum / cnt[None] - u[:, :, lanes]
        y = _dot(pooled.reshape(bb * tt, POOL_GW).astype(BF16), w_ref[g])
        y_ref[:, lanes] = y * sc_ref[:, lanes]
    ext[:, 0:POOL_HALO, :] = ext[:, tt:tt + POOL_HALO, :]


def _pool_mix(u, past, w_pool, scale, *, row0, b, t, bb, tt, pos0, prev):
    return _seq_call(
        functools.partial(_pool_kernel, bb=bb, tt=tt, pos0=pos0), name="pool_mix",
        row0=row0, b=b, t=t, bb=bb, tt=tt, n_rows=u.shape[0],
        seq_ins=[u], seq_widths=[POOL_W],
        other_ins=[past, w_pool, scale],
        other_specs=[pl.BlockSpec((bb, POOL_HALO, POOL_W), lambda i, j: (i, 0, 0)), _full(w_pool, 2), _full(scale, 2)],
        out_widths=[POOL_W], extra_out_shapes=[], extra_out_specs=[],
        scratch=[pltpu.VMEM((bb, POOL_HALO + tt, POOL_W), F32)], prev_outs=prev)[0]


def _mlstm_kernel(q_ref, k_ref, v_ref, o_ref, g_ref, c0_ref, n0_ref, m0_ref, hng_ref, *rest, bb, l, nc, n_prev):
    y_ref, c_out, n_out, m_out, cs, ns, ms = rest[n_prev:]
    c = pl.program_id(1)

    @pl.when(c == 0)
    def _():
        cs[...] = c0_ref[...]
        ns[...] = n0_ref[...]
        ms[...] = m0_ref[...]

    row = lax.broadcasted_iota(I32, (l, l), 0)
    col = lax.broadcasted_iota(I32, (l, l), 1)
    causal = row >= col
    scale = MLSTM_DH ** -0.5
    padr = LANES - l if l < 16 else 0

    assert nc == 1
    tril = causal.astype(BF16)
    lane = lax.broadcasted_iota(I32, (l, LANES), 1)
    gate_cols, gate_rows = {}, {}

    def lockstep(chains):
        live = list(chains)
        while live:
            still = []
            for ch in live:
                try:
                    next(ch)
                    still.append(ch)
                except StopIteration:
                    pass
            live = still

    def gate_chain(b):
        gates = g_ref[b * l:(b + 1) * l, :]
        bcum = _dot_exact_lhs(tril, _log_sigmoid(gates))
        yield
        cols = jnp.where(lane < MLSTM_HEADS, gates, bcum)
        gate_cols[b] = cols
        gate_rows[b] = _rows_of(cols)
        yield

    def head_chain(b, h):
        rows_b = pl.ds(b * l, l)
        hl = slice(h * MLSTM_DH, (h + 1) * MLSTM_DH)
        cols, rows = gate_cols[b], gate_rows[b]
        i_col = cols[:, h:h + 1]
        b_col = cols[:, MLSTM_HEADS + h:MLSTM_HEADS + h + 1]
        i_row = rows[h:h + 1, :]
        b_row = rows[MLSTM_HEADS + h:MLSTM_HEADS + h + 1, :]
        c_prev, n_prev_, m_prev = cs[b, h], ns[b, h], ms[b, h][:, 0:1]
        qh = q_ref[rows_b, hl] * scale
        kh = k_ref[rows_b, hl]
        vh = v_ref[rows_b, hl]

        dmat = jnp.where(causal, b_col - b_row + i_row, -jnp.inf)
        inter = b_col + m_prev
        qk = lax.dot_general(qh, kh, (((1,), (1,)), ((), ())), preferred_element_type=F32)
        q_c = _dot(qh, c_prev)
        m_t = jnp.maximum(inter, jnp.max(dmat, axis=-1, keepdims=True))
        yield
        s = qk * jnp.exp(dmat - m_t)
        w_inter = jnp.exp(inter - m_t)
        num = _dot(s, vh) + w_inter * q_c
        den = jnp.sum(s, -1, keepdims=True) + w_inter * jnp.sum(qh * n_prev_, -1, keepdims=True)
        yield
        bound = jnp.maximum(jnp.abs(den), jnp.exp(-m_t))
        hh = num / bound
        mu = jnp.mean(hh, -1, keepdims=True)
        yield
        hc = hh - mu
        var = jnp.mean(hc * hc, -1, keepdims=True)
        yield
        hn = hc * lax.rsqrt(var + LN_EPS) * hng_ref[:, hl]
        y_ref[rows_b, hl] = jax.nn.sigmoid(o_ref[rows_b, hl]) * hn

        b_end = b_col[l - 1:l, :]
        gcol = b_end - b_col + i_col
        m_new = jnp.maximum(b_end + m_prev, jnp.max(gcol, axis=0, keepdims=True))
        kw = kh * jnp.exp(gcol - m_new)
        decay = jnp.exp(b_end + m_prev - m_new)
        if padr:
            zpad = jnp.zeros((padr, MLSTM_DH), F32)
            kv = lax.dot_general(jnp.concatenate([kw, zpad], 0), jnp.concatenate([vh, zpad], 0),
                                 (((0,), (0,)), ((), ())), preferred_element_type=F32)
        else:
            kv = lax.dot_general(kw, vh, (((0,), (0,)), ((), ())), preferred_element_type=F32)
        yield
        cs[b, h] = decay * c_prev + kv
        ns[b, h] = decay * n_prev_ + jnp.sum(kw, axis=0, keepdims=True)
        ms[b, h] = jnp.broadcast_to(m_new, (1, LANES))

    lockstep(gate_chain(b) for b in range(bb))
    for b0 in range(0, bb, MLSTM_LOCKSTEP_SEQS):
        lockstep(head_chain(b, h) for b in range(b0, min(bb, b0 + MLSTM_LOCKSTEP_SEQS)) for h in range(MLSTM_HEADS))

    @pl.when(c == pl.num_programs(1) - 1)
    def _():
        c_out[...] = cs[...]
        n_out[...] = ns[...]
        m_out[...] = ms[...]


def _mlstm(q, k, v, o, g, c0, n0, m0, hn_g, *, row0, b, t, bb, l, nc, prev):
    st_c = pl.BlockSpec((bb, MLSTM_HEADS, MLSTM_DH, MLSTM_DH), lambda i, j: (i, 0, 0, 0))
    st_v = pl.BlockSpec((bb, MLSTM_HEADS, 1, LANES), lambda i, j: (i, 0, 0, 0))
    return _seq_call(
        functools.partial(_mlstm_kernel, bb=bb, l=l, nc=nc), name="mlstm",
        row0=row0, b=b, t=t, bb=bb, tt=nc * l, n_rows=q.shape[0],
        seq_ins=[q, k, v, o, g], seq_widths=[MLSTM_W] * 4 + [LANES],
        other_ins=[c0, n0, m0, hn_g], other_specs=[st_c, st_v, st_v, _full(hn_g, 2)],
        out_widths=[MLSTM_W],
        extra_out_shapes=[jax.ShapeDtypeStruct((b, MLSTM_HEADS, MLSTM_DH, MLSTM_DH), F32),
                          jax.ShapeDtypeStruct((b, MLSTM_HEADS, 1, LANES), F32),
                          jax.ShapeDtypeStruct((b, MLSTM_HEADS, 1, LANES), F32)],
        extra_out_specs=[st_c, st_v, st_v],
        scratch=[pltpu.VMEM((bb, MLSTM_HEADS, MLSTM_DH, MLSTM_DH), F32),
                 pltpu.VMEM((bb, MLSTM_HEADS, 1, LANES), F32),
                 pltpu.VMEM((bb, MLSTM_HEADS, 1, LANES), F32)],
        prev_outs=prev)


def _mix_out_kernel(*refs, n_parts, n_x, tiles_p):
    parts = refs[:n_parts]
    ws = refs[n_parts:2 * n_parts]
    x_refs = refs[2 * n_parts:2 * n_parts + n_x]
    (bo_ref, g_ref, b_ref, wr_ref, br_ref,
     x1_ref, x1t_ref, route_ref, gate_ref, cnt_ref, carry) = refs[2 * n_parts + n_x:]
    i = pl.program_id(0)

    @pl.when(i == 0)
    def _():
        carry[...] = jnp.zeros_like(carry)

    mix = bo_ref[...]
    for a_ref, w_ref in zip(parts, ws):
        mix = mix + _dot(a_ref[...].astype(BF16), w_ref[...])
    x1 = _layer_norm(DN_ALPHA * _token_rows(x_refs, tiles_p) + mix, g_ref[...], b_ref[...])
    x1_ref[...] = x1
    tm = x1.shape[0]
    _store_token_tiles(x1t_ref, x1)

    logits = _dot(x1.astype(BF16), wr_ref[...]) + br_ref[...]
    lane = lax.broadcasted_iota(I32, (tm, LANES), 1)
    work = logits
    picks, vals = [], []
    onehot = jnp.zeros((tm, LANES), F32)
    for _ in range(TOP_K):
        mx = jnp.max(work, axis=-1, keepdims=True)
        pick = jnp.min(jnp.where(work == mx, lane, LANES), axis=-1, keepdims=True)
        hit = lane == pick
        onehot = jnp.where(hit, 1.0, onehot)
        work = jnp.where(hit, -jnp.inf, work)
        picks.append(pick)
        vals.append(mx)
    exps = [jnp.exp(v - vals[0]) for v in vals]
    tot = exps[0] + exps[1] + exps[2] + exps[3]

    rr = lax.broadcasted_iota(I32, (tm, tm), 0)
    cc = lax.broadcasted_iota(I32, (tm, tm), 1)
    before = _dot((rr > cc).astype(BF16), onehot.astype(BF16)) + carry[...]
    route = jnp.zeros((tm, LANES), F32)
    gate_o = jnp.zeros((tm, LANES), F32)
    for kk in range(TOP_K):
        rk = jnp.sum(jnp.where(lane == picks[kk], before, 0.0), axis=-1, keepdims=True)
        route = jnp.where(lane == kk, picks[kk].astype(F32), route)
        route = jnp.where(lane == TOP_K + kk, rk, route)
        gate_o = jnp.where(lane == kk, exps[kk] / tot, gate_o)
    route_ref[...] = _rows_of(route).astype(I32)
    gate_ref[...] = gate_o
    carry[...] = carry[...] + jnp.sum(onehot, axis=0, keepdims=True)
    cnt_ref[...] = carry[...]


def _mix_out(parts, ws, x_srcs, b_out, ln_g, ln_b, w_r, b_r):
    tm = TOKEN_TILE
    x_specs, tiles_p, n = _token_specs(x_srcs)
    row = lambda c: pl.BlockSpec((tm, c), lambda i: (i, 0))
    small = [b_out, ln_g, ln_b, w_r, b_r]
    return pl.pallas_call(
        functools.partial(_mix_out_kernel, n_parts=len(parts), n_x=len(x_srcs), tiles_p=tiles_p),
        grid=(n // tm,),
        in_specs=[row(p.shape[1]) for p in parts] + [_full(w, 1) for w in ws] + x_specs
        + [_full(a, 1) for a in small],
        out_specs=[row(D_MODEL), pl.BlockSpec((tm * TILE_ROWS, LANES), lambda i: (i, 0)),
                   pl.BlockSpec((SUBLANES, tm), lambda i: (0, i)), row(LANES),
                   pl.BlockSpec((1, LANES), lambda i: (0, 0))],
        out_shape=[
            jax.ShapeDtypeStruct((n, D_MODEL), F32),
            jax.ShapeDtypeStruct((n * TILE_ROWS, LANES), F32),
            jax.ShapeDtypeStruct((SUBLANES, n), I32),
            jax.ShapeDtypeStruct((n, LANES), F32),
            jax.ShapeDtypeStruct((1, LANES), F32),
        ],
        scratch_shapes=[pltpu.VMEM((1, LANES), F32)],
        compiler_params=_cparams(("arbitrary",)),
        name="mix_out_router",
    )(*parts, *ws, *x_srcs, *small)


MOE_CHUNKS = 4
MOE_TAB_RING = 4
MOE_PAD_TILES = 2


def _moe_kernel(w_tile, w_exp, w_lo, w_hi, w_first, w_last, w_newexp, w_valid,
                tab_hbm, x_hbm, wgu_ref, bgu_ref, wdn_ref, bdn_ref, yk_hbm,
                wgu_b, wdn_b, h_scr, xbuf, ybuf, tab, tsem, gsem, ssem, *, n_tiles, tm):
    j = pl.program_id(0)
    t = w_tile[j]
    b2 = t & 1
    unroll = 8

    def tab_slot(tile):
        return (tile + MOE_PAD_TILES) & (MOE_TAB_RING - 1)

    def tab_copy(tile):
        return pltpu.make_async_copy(tab_hbm.at[pl.ds((tile + MOE_PAD_TILES) * (2 * tm), 2 * tm)],
                                     tab.at[pl.ds(tab_slot(tile) * (2 * tm), 2 * tm)], tsem.at[tab_slot(tile)])

    def token(ref, tok):
        return ref.at[pl.ds(pl.multiple_of(tok * TILE_ROWS, TILE_ROWS), TILE_ROWS), :]

    def gather_copy(tile, r):
        src = tab[tab_slot(tile) * (2 * tm) + r]
        return pltpu.make_async_copy(token(x_hbm, src), token(xbuf.at[tile & 1], r), gsem.at[tile & 1])

    def scatter_copy(tile, r):
        dst = tab[tab_slot(tile) * (2 * tm) + tm + r]
        return pltpu.make_async_copy(token(ybuf.at[tile & 1], r), token(yk_hbm, dst), ssem.at[tile & 1])

    def start_all(copy, tile):
        def body(i, carry):
            for u in range(unroll):
                copy(tile, i * unroll + u).start()
            return carry

        lax.fori_loop(0, tm // unroll, body, 0)

    def wait_gather(tile):
        pltpu.make_async_copy(x_hbm.at[pl.ds(0, tm * TILE_ROWS), :], xbuf.at[tile & 1], gsem.at[tile & 1]).wait()

    def wait_scatter(tile):
        pltpu.make_async_copy(ybuf.at[tile & 1], yk_hbm.at[pl.ds(0, tm * TILE_ROWS), :], ssem.at[tile & 1]).wait()

    @pl.when(j == 0)
    def _():
        for tile in (-2, -1, 0):
            cp = tab_copy(tile)
            cp.start()
            cp.wait()
        tab_copy(1).start()
        ybuf[...] = jnp.zeros(ybuf.shape, F32)
        start_all(scatter_copy, -2)
        start_all(gather_copy, 0)

    @pl.when(w_newexp[j] == 1)
    def _():
        wgu_b[...] = wgu_ref[0, 0].astype(BF16)
        wdn_b[...] = wdn_ref[0, 0].astype(BF16)

    chunk = D_FF // MOE_CHUNKS
    rows_per_chunk = tm // MOE_CHUNKS

    def experts(first_item):
        if first_item:
            wait_gather(t)
            tab_copy(t + 1).wait()
            tab_copy(t + 2).start()
        x = _load_token_tiles(xbuf.at[b2], tm, BF16)
        bgu = bgu_ref[0, 0]
        for c in range(MOE_CHUNKS):
            cg = slice(c * chunk, (c + 1) * chunk)
            cu = slice(D_FF + c * chunk, D_FF + (c + 1) * chunk)
            g = jnp.minimum(_dot(x, wgu_b[:, cg]) + bgu[:, cg], SWIGLU_LIMIT)
            u = jnp.clip(_dot(x, wgu_b[:, cu]) + bgu[:, cu], -SWIGLU_LIMIT, SWIGLU_LIMIT)
            h_scr[:, cg] = ((u + 1.0) * g * jax.nn.sigmoid(SWIGLU_ALPHA * g)).astype(BF16)
            if first_item:
                for r in range(c * rows_per_chunk, (c + 1) * rows_per_chunk):
                    gather_copy(t + 1, r).start()
        if first_item:
            wait_scatter(t - 2)
        else:
            row = lax.broadcasted_iota(I32, (tm, 1), 0)
            mine = (row >= w_lo[j]) & (row < w_hi[j])
        h = h_scr[...]
        bdn = bdn_ref[0, 0]
        out = ybuf.at[b2]
        for c in range(MOE_CHUNKS):
            cy = slice(c * chunk, (c + 1) * chunk)
            y = _dot(h, wdn_b[:, cy]) + bdn[:, cy]
            for s in range(chunk // LANES):
                piece = y[:, s * LANES:(s + 1) * LANES]
                rows = pl.ds(c * (chunk // LANES) + s, tm, stride=TILE_ROWS)
                out[rows, :] = piece if first_item else jnp.where(mine, piece, out[rows, :])
            if first_item:
                for r in range(c * rows_per_chunk, (c + 1) * rows_per_chunk):
                    scatter_copy(t - 1, r).start()

    @pl.when(w_first[j] == 1)
    def _():
        experts(True)

    @pl.when((w_valid[j] == 1) & (w_first[j] == 0))
    def _():
        experts(False)

    @pl.when((w_last[j] == 1) & (t == n_tiles - 1))
    def _():
        start_all(scatter_copy, t)
        wait_scatter(t - 1)
        wait_scatter(t)
        wait_gather(t + 1)
        tab_copy(t + 2).wait()


def _moe_experts(l, work, tab, x1t, w_gu, b_gu, w_dn, b_dn):
    n = x1t.shape[0] // TILE_ROWS
    tm = MOE_TILE
    n_tiles = n * TOP_K // tm
    n_work = work[0].shape[0]
    by_expert = lambda j, wt, we, *_: (l, we[j], 0, 0)
    grid_spec = pltpu.PrefetchScalarGridSpec(
        num_scalar_prefetch=8,
        grid=(n_work,),
        in_specs=[
            _ANY, _ANY,
            pl.BlockSpec((1, 1, D_MODEL, 2 * D_FF), by_expert),
            pl.BlockSpec((1, 1, 1, 2 * D_FF), by_expert),
            pl.BlockSpec((1, 1, D_FF, D_MODEL), by_expert),
            pl.BlockSpec((1, 1, 1, D_MODEL), by_expert),
        ],
        out_specs=_ANY,
        scratch_shapes=[
            pltpu.VMEM((D_MODEL, 2 * D_FF), BF16), pltpu.VMEM((D_FF, D_MODEL), BF16),
            pltpu.VMEM((tm, D_FF), BF16),
            pltpu.VMEM((2, tm * TILE_ROWS, LANES), F32),
            pltpu.VMEM((2, tm * TILE_ROWS, LANES), F32),
            pltpu.SMEM((MOE_TAB_RING * 2 * tm,), I32),
            pltpu.SemaphoreType.DMA((MOE_TAB_RING,)), pltpu.SemaphoreType.DMA((2,)), pltpu.SemaphoreType.DMA((2,)),
        ],
    )
    return pl.pallas_call(
        functools.partial(_moe_kernel, n_tiles=n_tiles, tm=tm),
        grid_spec=grid_spec,
        out_shape=jax.ShapeDtypeStruct(((TOP_K * n + MOE_PAD_TILES * tm) * TILE_ROWS, LANES), F32),
        compiler_params=_cparams(("arbitrary",)),
        name="moe_experts",
    )(*work, tab, x1t, w_gu, b_gu[:, :, None, :], w_dn, b_dn[:, :, None, :])


def _moe_work_items(counts, n_slots):
    tm = MOE_TILE
    n_tiles = n_slots // tm
    n_work = n_tiles + N_EXPERTS
    ends = jnp.cumsum(counts)
    offs = ends - counts
    cand = jnp.concatenate([jnp.arange(n_tiles, dtype=I32) * tm, offs])
    ci = jnp.arange(n_work, dtype=I32)
    less = (cand[None, :] < cand[:, None]) | ((cand[None, :] == cand[:, None]) & (ci[None, :] < ci[:, None]))
    order = jnp.sum(less.astype(I32), axis=1)
    starts = jnp.sum(jnp.where(order[None, :] == ci[:, None], cand[None, :], 0), axis=1)
    stops = jnp.concatenate([starts[1:], jnp.array([n_slots], I32)])
    valid = stops > starts
    vpos = jnp.cumsum(valid.astype(I32)) - 1
    n_valid = jnp.sum(valid.astype(I32))
    want = jnp.minimum(ci, n_valid - 1)
    pick = valid[None, :] & (vpos[None, :] == want[:, None])
    starts = jnp.sum(jnp.where(pick, starts[None, :], 0), axis=1)
    stops = jnp.sum(jnp.where(pick, stops[None, :], 0), axis=1)
    is_valid = ci < n_valid
    tile = starts // tm
    expert = jnp.minimum(jnp.sum((ends[None, :] <= starts[:, None]).astype(I32), axis=1), N_EXPERTS - 1)
    prev = lambda a: jnp.concatenate([jnp.array([-1], I32), a[:-1]])
    nxt = lambda a: jnp.concatenate([a[1:], jnp.array([-1], I32)])
    first = (tile != prev(tile)) & is_valid
    last = ((tile != nxt(tile)) | (ci == n_valid - 1)) & is_valid
    newexp = (expert != prev(expert)) & is_valid
    as_i = lambda a: a.astype(I32)
    return offs, (tile, expert, starts - tile * tm, stops - tile * tm, as_i(first), as_i(last), as_i(newexp),
                  as_i(is_valid))


def _ffn_ln_kernel(x_ref, *rest, split_tile):
    yk_refs = rest[:TOP_K]
    gate_ref, g_ref, b_ref = rest[TOP_K:TOP_K + 3]
    out_refs = rest[TOP_K + 3:]
    tm = x_ref.shape[0]
    gate = gate_ref[...]
    ff = gate[:, 0:1] * _load_token_tiles(yk_refs[0], tm)
    for kk in range(1, TOP_K):
        ff = ff + gate[:, kk:kk + 1] * _load_token_tiles(yk_refs[kk], tm)
    out = _layer_norm(DN_ALPHA * x_ref[...] + ff, g_ref[...], b_ref[...])
    if split_tile is None:
        out_refs[0][...] = out
    else:
        @pl.when(pl.program_id(0) < split_tile)
        def _():
            out_refs[0][...] = out.reshape(out_refs[0].shape)

        @pl.when(pl.program_id(0) >= split_tile)
        def _():
            out_refs[1][...] = out


def _ffn_ln(x, yk, gate, ln_g, ln_b, split=None):
    n = x.shape[0]
    tm = TOKEN_TILE
    pick = lambda kk: pl.BlockSpec((tm * TILE_ROWS, LANES), lambda i: (kk * (n // tm) + i, 0))
    row = lambda c: pl.BlockSpec((tm, c), lambda i: (i, 0))
    vec = pl.BlockSpec((1, D_MODEL), lambda i: (0, 0))
    if split is None:
        split_tile = None
        out_specs = row(D_MODEL)
        out_shape = jax.ShapeDtypeStruct((n, D_MODEL), F32)
    else:
        b, t = split
        chunk = tm // b
        split_tile = t // chunk
        out_specs = [pl.BlockSpec((b, chunk, D_MODEL), lambda i: (0, jnp.minimum(i, split_tile - 1), 0)),
                     pl.BlockSpec((tm, D_MODEL), lambda i: (jnp.maximum(i - split_tile, 0), 0))]
        out_shape = [jax.ShapeDtypeStruct((b, t, D_MODEL), F32), jax.ShapeDtypeStruct((n - b * t, D_MODEL), F32)]
    return pl.pallas_call(
        functools.partial(_ffn_ln_kernel, split_tile=split_tile),
        grid=(n // tm,),
        in_specs=[row(D_MODEL)] + [pick(kk) for kk in range(TOP_K)] + [row(LANES), vec, vec],
        out_specs=out_specs,
        out_shape=out_shape,
        compiler_params=_cparams(("arbitrary",)),
        name="ffn_ln",
    )(x, *([yk] * TOP_K), gate, ln_g, ln_b)


def _moe_block(l, x1, x1t, route_t, gate, counts, ln_g, ln_b, w_gu, b_gu, w_dn, b_dn, split=None):
    n = x1.shape[0]
    n_slots = n * TOP_K
    tm = MOE_TILE
    offs, work = _moe_work_items(counts, n_slots)
    idx_t, rank_t = route_t[:TOP_K], route_t[TOP_K:]
    experts = jnp.arange(N_EXPERTS, dtype=I32)
    slot_t = rank_t + jnp.sum(jnp.where(idx_t[..., None] == experts, offs, 0), axis=-1)
    pair = jnp.argsort(slot_t.reshape(-1)).astype(I32)
    spare = n_slots + jnp.arange(MOE_PAD_TILES * tm, dtype=I32).reshape(MOE_PAD_TILES, tm)
    first = jnp.arange(MOE_PAD_TILES * tm, dtype=I32).reshape(MOE_PAD_TILES, tm) % tm
    src = jnp.concatenate([first, (pair % n).reshape(-1, tm), first], axis=0)
    dst = jnp.concatenate([spare, pair.reshape(-1, tm), spare], axis=0)
    tab = jnp.concatenate([src, dst], axis=1).reshape(-1)
    yk = _moe_experts(l, work, tab, x1t, w_gu, b_gu, w_dn, b_dn)
    return _ffn_ln(x1, yk, gate, ln_g[l][None], ln_b[l][None], split)


def _cv_in_kernel(x_ref, w_ref, b_ref, g_ref):
    xb = x_ref[...].astype(BF16)
    a = _dot(xb, w_ref[:, :CONV_W]) + b_ref[:, :CONV_W]
    gate = _dot(xb, w_ref[:, CONV_W:]) + b_ref[:, CONV_W:]
    _store_token_tiles(g_ref, a * jax.nn.sigmoid(gate))


def _cv_in_proj(x, w, b):
    n = x.shape[0]
    tm = TOKEN_TILE
    return pl.pallas_call(
        _cv_in_kernel,
        grid=(n // tm,),
        in_specs=[pl.BlockSpec((tm, D_MODEL), lambda i: (i, 0)), _full(w, 1), _full(b, 1)],
        out_specs=pl.BlockSpec((tm * TILE_ROWS, LANES), lambda i: (i, 0)),
        out_shape=jax.ShapeDtypeStruct((n * TILE_ROWS, LANES), F32),
        compiler_params=_cparams(("parallel",)),
        name="cv_in_proj",
    )(x, w, b)


CONV_GROUP = 16


def _dwconv_kernel(g_ref, past_ref, w_ref, wb_ref, lg_ref, lb_ref, *rest, bb, tt, n_prev):
    c_ref, ext, conv = rest[n_prev:]
    j = pl.program_id(1)
    halo = CONV_HALO * TILE_ROWS

    @pl.when(j == 0)
    def _():
        ext[:, 0:halo, :] = past_ref[...]

    ext[:, halo:, :] = g_ref[...].reshape(bb, tt * TILE_ROWS, LANES)
    group = min(CONV_GROUP, tt)
    first_tap = CONV_HALO - CONV_STATE

    for b in range(bb):
        def per_group(gi, carry):
            t0 = gi * group
            acc = None
            for d in range(CONV_K):
                rows = pl.ds(pl.multiple_of((first_tap + d + t0) * TILE_ROWS, TILE_ROWS), group * TILE_ROWS)
                window = ext[b, rows, :].reshape(group, TILE_ROWS, LANES)
                term = window * w_ref[d * TILE_ROWS:(d + 1) * TILE_ROWS, :][None]
                acc = term if acc is None else acc + term
            out_rows = pl.ds(pl.multiple_of((b * tt + t0) * TILE_ROWS, TILE_ROWS), group * TILE_ROWS)
            conv[out_rows, :] = acc.reshape(group * TILE_ROWS, LANES)
            return carry

        lax.fori_loop(0, tt // group, per_group, 0)

    c = _layer_norm(_load_token_tiles(conv, bb * tt) + wb_ref[...], lg_ref[...], lb_ref[...])
    c_ref[...] = c * jax.nn.sigmoid(c)
    ext[:, 0:halo, :] = ext[:, tt * TILE_ROWS:tt * TILE_ROWS + halo, :]


def _dwconv(g, past, dw_w, dw_b, ln_g, ln_b, *, row0, b, t, bb, tt, prev):
    vec = pl.BlockSpec((1, CONV_W), lambda i, j: (0, 0))
    return _seq_call(
        functools.partial(_dwconv_kernel, bb=bb, tt=tt), name="dwconv_ln_swish",
        row0=row0, b=b, t=t, bb=bb, tt=tt, n_rows=g.shape[0] // TILE_ROWS,
        seq_ins=[g], seq_widths=[None],
        other_ins=[past, dw_w, dw_b, ln_g, ln_b],
        other_specs=[pl.BlockSpec((bb, CONV_HALO * TILE_ROWS, LANES), lambda i, j: (i, 0, 0)), _full(dw_w, 2),
                     vec, vec, vec],
        out_widths=[CONV_W], extra_out_shapes=[], extra_out_specs=[],
        scratch=[pltpu.VMEM((bb, (CONV_HALO + tt) * TILE_ROWS, LANES), F32),
                 pltpu.VMEM((bb * tt * TILE_ROWS, LANES), F32)],
        prev_outs=prev)[0]


def _route_args(w_router, b_router, l):
    w_r = jnp.pad(w_router[l], ((0, 0), (0, LANES - N_EXPERTS))).astype(BF16)
    b_r = jnp.pad(b_router[l], (0, LANES - N_EXPERTS), constant_values=NEG_BIG)[None]
    return w_r, b_r


def kernel(x_prompt, x_sample, state_pool, state_mlstm_C, state_mlstm_n, state_mlstm_m, state_conv, w_in_pm, b_if, w_pool, pool_scale, hn_g, w_out_pm, b_out_pm, w_in_cv, b_in_cv, dw_w, dw_b, cln_g, cln_b, w_out_cv, b_out_cv, ln_mix_g, ln_mix_b, ln_ffn_g, ln_ffn_b, w_router, b_router, w_gu, b_gu, w_dn, b_dn):
    bp, tp, _ = x_prompt.shape
    bs, ts, _ = x_sample.shape
    n_p = bp * tp
    n = n_p + bs * ts
    x_srcs = [x_prompt, x_sample.reshape(bs * ts, D_MODEL)]
    chunk = TOKEN_TILE // bp
    assert chunk == MLSTM_CHUNK and tp % chunk == 0 and chunk >= max(POOL_STATE, CONV_STATE)

    groups = (
        dict(row0=0, b=bp, t=tp, pos0=0, pool_bb=bp, pool_tt=chunk, l=chunk, mlstm_bb=bp, mlstm_nc=1, conv_bb=bp,
             conv_tt=chunk, chunk_major=True,
             past_pool=jnp.zeros((bp, POOL_STATE, POOL_W), F32),
             c0=jnp.zeros((bp, MLSTM_HEADS, MLSTM_DH, MLSTM_DH), F32),
             n0=jnp.zeros((bp, MLSTM_HEADS, MLSTM_DH), F32),
             m0=jnp.zeros((bp, MLSTM_HEADS), F32),
             past_conv=jnp.zeros((bp, CONV_STATE, CONV_W), F32)),
        dict(row0=n_p, b=bs, t=ts, pos0=PAST_LEN, pool_bb=16, pool_tt=ts, l=ts, mlstm_bb=8, mlstm_nc=1, conv_bb=8,
             conv_tt=ts, chunk_major=False,
             past_pool=state_pool[0], c0=state_mlstm_C[0], n0=state_mlstm_n[0], m0=state_mlstm_m[0],
             past_conv=state_conv[0]),
    )

    def tail_state(past, a, gr, keep, tiled=False):
        b, t, row0 = gr["b"], gr["t"], gr["row0"]
        rpt = TILE_ROWS if tiled else 1
        width = a.shape[-1] * rpt
        if gr["chunk_major"]:
            ends = [row0 + (t - chunk) * b + (i + 1) * chunk for i in range(b)]
            parts = [a[(e - keep) * rpt:e * rpt] for e in ends]
            return jnp.stack(parts).reshape(b, keep, width)
        new = a[row0 * rpt:(row0 + b * t) * rpt].reshape(b, t, width)
        return jnp.concatenate([past[:, t:], new], axis=1)

    w_in = w_in_pm[0]
    w_main = w_in[:, :POOL_W + 4 * MLSTM_W].astype(BF16)
    n_gate = 2 * MLSTM_HEADS
    w_gate = jnp.pad(w_in[:, POOL_W + 4 * MLSTM_W:], ((0, 0), (0, LANES - n_gate))).astype(BF16)
    b_gate = jnp.pad(b_if[0], (0, LANES - n_gate))[None]
    w_out = w_out_pm[0].astype(BF16)
    w_r, b_r = _route_args(w_router, b_router, 0)

    u, q, k, v, o, g = _pm_in_proj(x_srcs, w_main, w_gate, b_gate)
    y_pool = jnp.zeros((n, POOL_W), F32)
    y_m = jnp.zeros((n, MLSTM_W), F32)
    states = []
    for gr in groups:
        b, t = gr["b"], gr["t"]
        past = jnp.pad(gr["past_pool"], ((0, 0), (POOL_HALO - POOL_STATE, 0), (0, 0)))
        y_pool = _pool_mix(u, past, w_pool[0].astype(BF16), pool_scale[0][None], row0=gr["row0"], b=b, t=t,
                           bb=gr["pool_bb"], tt=gr["pool_tt"], pos0=gr["pos0"],
                           prev=[y_pool])
        n0 = gr["n0"][:, :, None, :]
        m0 = jnp.broadcast_to(gr["m0"][:, :, None, None], (b, MLSTM_HEADS, 1, LANES))
        y_m, c_new, n_new, m_new = _mlstm(q, k, v, o, g, gr["c0"], n0, m0, hn_g[0][None], row0=gr["row0"], b=b, t=t,
                                          bb=gr["mlstm_bb"], l=gr["l"], nc=gr["mlstm_nc"], prev=[y_m])
        states.append((tail_state(gr["past_pool"], u, gr, POOL_STATE), c_new, n_new[:, :, 0, :], m_new[:, :, 0, 0]))
    x1, x1t, route_t, gate, counts = _mix_out([y_pool, y_m], [w_out[:POOL_W], w_out[POOL_W:]], x_srcs,
                                              b_out_pm[0][None], ln_mix_g[0][None], ln_mix_b[0][None], w_r, b_r)
    x2 = _moe_block(0, x1, x1t, route_t, gate, counts[0, :N_EXPERTS].astype(I32), ln_ffn_g, ln_ffn_b,
                    w_gu, b_gu, w_dn, b_dn)

    w_r, b_r = _route_args(w_router, b_router, 1)
    dw_w_p = jnp.pad(dw_w[0], ((0, CONV_HALO - CONV_K), (0, 0))).reshape(CONV_HALO * TILE_ROWS, LANES)
    gl = _cv_in_proj(x2, w_in_cv[0].astype(BF16), b_in_cv[0][None])
    c, conv_states = jnp.zeros((n, CONV_W), F32), []
    for gr in groups:
        past = jnp.pad(gr["past_conv"], ((0, 0), (CONV_HALO - CONV_STATE, 0), (0, 0)))
        past = past.reshape(gr["b"], CONV_HALO * TILE_ROWS, LANES)
        c = _dwconv(gl, past, dw_w_p, dw_b[0][None], cln_g[0][None], cln_b[0][None], row0=gr["row0"], b=gr["b"],
                    t=gr["t"], bb=gr["conv_bb"], tt=gr["conv_tt"], prev=[c])
        conv_states.append(tail_state(gr["past_conv"], gl, gr, CONV_STATE, tiled=True))
    x3, x3t, route_t, gate, counts = _mix_out([c], [w_out_cv[0].astype(BF16)], [x2], b_out_cv[0][None],
                                              ln_mix_g[1][None], ln_mix_b[1][None], w_r, b_r)
    y_p, y_s = _moe_block(1, x3, x3t, route_t, gate, counts[0, :N_EXPERTS].astype(I32), ln_ffn_g, ln_ffn_b,
                          w_gu, b_gu, w_dn, b_dn, split=(bp, tp))

    (pool_p, c_p, n_p_, m_p), (pool_s, c_s, n_s, m_s) = states
    return (y_p, y_s.reshape(bs, ts, D_MODEL),
            pool_p[None], pool_s[None], c_p[None], c_s[None], n_p_[None], n_s[None], m_p[None], m_s[None],
            conv_states[0][None], conv_states[1][None])
```

```python
import functools

import jax
import jax.numpy as jnp
from jax import lax
from jax.experimental import pallas as pl
from jax.experimental.pallas import tpu as pltpu

F32 = jnp.float32
BF16 = jnp.bfloat16
I32 = jnp.int32

D_MODEL = 1024
DEPTH = 2
POOL_W = 512
POOL_WINDOWS = (2, 4, 8, 16)
POOL_GW = 128
POOL_STATE = 15
POOL_HALO = 16
MLSTM_HEADS = 4
MLSTM_W = 512
MLSTM_DH = 128
MLSTM_CHUNK = 64
MLSTM_LOCKSTEP_SEQS = 2
CONV_W = 1024
CONV_K = 31
CONV_STATE = 30
CONV_HALO = 32
N_EXPERTS = 32
TOP_K = 4
D_FF = 1024
SWIGLU_LIMIT = 7.0
SWIGLU_ALPHA = 1.702
LN_EPS = 1e-5
DN_ALPHA = (2 * DEPTH) ** 0.25
PAST_LEN = 16384

LANES = 128
SUBLANES = 8
TOKEN_TILE = 512
MOE_TILE = 512
NEG_BIG = -1e30
VMEM_LIMIT = 56 * 1024 * 1024


def _cparams(sem, vmem=VMEM_LIMIT):
    return pltpu.CompilerParams(dimension_semantics=sem, vmem_limit_bytes=vmem)


def _layer_norm(x, g, b):
    mu = jnp.mean(x, -1, keepdims=True)
    xc = x - mu
    var = jnp.mean(xc * xc, -1, keepdims=True)
    return xc * lax.rsqrt(var + LN_EPS) * g + b


def _log_sigmoid(x):
    return jnp.minimum(x, 0.0) - jnp.log1p(jnp.exp(-jnp.abs(x)))


def _dot(a, b):
    return jnp.dot(a, b, preferred_element_type=F32)


def _bf16_pieces(x):
    hi = x.astype(BF16)
    rest = x - hi.astype(F32)
    mid = rest.astype(BF16)
    return hi, mid, (rest - mid.astype(F32)).astype(BF16)


def _dot_exact_lhs(a, x, dims=(((1,), (0,)), ((), ()))):
    a = a.astype(BF16)
    return sum(lax.dot_general(a, p, dims, preferred_element_type=F32) for p in _bf16_pieces(x))


def _rows_of(cols):
    sel = (lax.broadcasted_iota(I32, (SUBLANES, LANES), 0) == lax.broadcasted_iota(I32, (SUBLANES, LANES), 1))
    return _dot_exact_lhs(sel, cols, (((1,), (1,)), ((), ())))


TILE_ROWS = D_MODEL // LANES


def _store_token_tiles(ref, val):
    tokens = val.shape[0]
    for s in range(TILE_ROWS):
        ref[pl.ds(s, tokens, stride=TILE_ROWS), :] = val[:, s * LANES:(s + 1) * LANES]


def _load_token_tiles(ref, tokens, dtype=F32):
    return jnp.concatenate([ref[pl.ds(s, tokens, stride=TILE_ROWS), :].astype(dtype) for s in range(TILE_ROWS)],
                           axis=1)


_ANY = pl.BlockSpec(memory_space=pl.ANY)


def _full(a, grid_rank):
    return pl.BlockSpec(a.shape, lambda *_: (0,) * a.ndim)


def _token_specs(x_srcs):
    tm = TOKEN_TILE
    if len(x_srcs) == 1:
        return [pl.BlockSpec((tm, D_MODEL), lambda i: (i, 0))], None, x_srcs[0].shape[0]
    x_p, x_s = x_srcs
    b, t, _ = x_p.shape
    chunk = tm // b
    tiles_p = t // chunk
    assert b * chunk == tm and tiles_p * chunk == t and x_s.shape[0] % tm == 0
    specs = [pl.BlockSpec((b, chunk, D_MODEL), lambda i: (0, jnp.minimum(i, tiles_p - 1), 0)),
             pl.BlockSpec((tm, D_MODEL), lambda i: (jnp.maximum(i - tiles_p, 0), 0))]
    return specs, tiles_p, b * t + x_s.shape[0]


def _token_rows(x_refs, tiles_p):
    if len(x_refs) == 1:
        return x_refs[0][...]
    x_p, x_s = x_refs
    return jnp.where(pl.program_id(0) < tiles_p, x_p[...].reshape(TOKEN_TILE, D_MODEL), x_s[...])


def _pm_in_kernel(*refs, n_x, tiles_p):
    x_refs = refs[:n_x]
    w_ref, wg_ref, bg_ref, u_ref, q_ref, k_ref, v_ref, o_ref, g_ref = refs[n_x:]
    xb = _token_rows(x_refs, tiles_p).astype(BF16)
    for j, out in enumerate((u_ref, q_ref, k_ref, v_ref, o_ref)):
        out[...] = _dot(xb, w_ref[:, j * 512:(j + 1) * 512])
    g_ref[...] = _dot(xb, wg_ref[...]) + bg_ref[...]


def _pm_in_proj(x_srcs, w_main, w_gate, b_gate):
    tm = TOKEN_TILE
    x_specs, tiles_p, n = _token_specs(x_srcs)
    row = lambda c: pl.BlockSpec((tm, c), lambda i: (i, 0))
    return pl.pallas_call(
        functools.partial(_pm_in_kernel, n_x=len(x_srcs), tiles_p=tiles_p),
        grid=(n // tm,),
        in_specs=x_specs + [_full(w_main, 1), _full(w_gate, 1), _full(b_gate, 1)],
        out_specs=[row(512)] * 5 + [row(LANES)],
        out_shape=[jax.ShapeDtypeStruct((n, 512), F32)] * 5 + [jax.ShapeDtypeStruct((n, LANES), F32)],
        compiler_params=_cparams(("arbitrary",)),
        name="pm_in_proj",
    )(*x_srcs, w_main, w_gate, b_gate)


def _seq_call(kernel, *, name, row0, b, t, bb, tt, n_rows, seq_ins, seq_widths, other_ins, other_specs,
              out_widths, extra_out_shapes, extra_out_specs, scratch, prev_outs, in_place=None):
    steps = t // tt
    blk0 = row0 // (bb * tt)

    def seq_spec(c):
        shape = (bb * tt * TILE_ROWS, LANES) if c is None else (bb * tt, c)
        return pl.BlockSpec(shape, lambda i, j: (blk0 + i * steps + j, 0))

    prev = list(prev_outs)
    first_prev = len(seq_ins) + len(other_ins)
    aliases = {first_prev + o: o for o in range(len(prev))}
    if in_place is not None:
        aliases[in_place] = 0
    return pl.pallas_call(
        functools.partial(kernel, n_prev=len(prev)),
        grid=(b // bb, steps),
        in_specs=[seq_spec(c) for c in seq_widths] + list(other_specs) + [_ANY] * len(prev),
        out_specs=[seq_spec(c) for c in out_widths] + list(extra_out_specs),
        out_shape=[jax.ShapeDtypeStruct((n_rows, c), F32) for c in out_widths] + list(extra_out_shapes),
        input_output_aliases=aliases,
        scratch_shapes=scratch,
        compiler_params=_cparams(("parallel", "arbitrary")),
        name=name,
    )(*seq_ins, *other_ins, *prev)


def _pool_kernel(u_ref, past_ref, w_ref, sc_ref, *rest, bb, tt, pos0, n_prev):
    y_ref, ext = rest[n_prev:]
    j = pl.program_id(1)

    @pl.when(j == 0)
    def _():
        ext[:, 0:POOL_HALO, :] = past_ref[...]

    u = u_ref[...].reshape(bb, tt, POOL_W)
    ext[:, POOL_HALO:POOL_HALO + tt, :] = u
    pos = pos0 + j * tt + lax.broadcasted_iota(I32, (tt, 1), 0)
    for g, w in enumerate(POOL_WINDOWS):
        lanes = slice(g * POOL_GW, (g + 1) * POOL_GW)
        wsum = ext[:, POOL_HALO:POOL_HALO + tt, lanes]
        for d in range(1, w):
            wsum = wsum + ext[:, POOL_HALO - d:POOL_HALO - d + tt, lanes]
        cnt = jnp.minimum(w, pos + 1).astype(F32)
        pooled = wsum / cnt[None] - u[:, :, lanes]
        y = _dot(pooled.reshape(bb * tt, POOL_GW).astype(BF16), w_ref[g])
        y_ref[:, lanes] = y * sc_ref[:, lanes]
    ext[:, 0:POOL_HALO, :] = ext[:, tt:tt + POOL_HALO, :]


def _pool_mix(u, past, w_pool, scale, *, row0, b, t, bb, tt, pos0, prev):
    return _seq_call(
        functools.partial(_pool_kernel, bb=bb, tt=tt, pos0=pos0), name="pool_mix",
        row0=row0, b=b, t=t, bb=bb, tt=tt, n_rows=u.shape[0],
        seq_ins=[u], seq_widths=[POOL_W],
        other_ins=[past, w_pool, scale],
        other_specs=[pl.BlockSpec((bb, POOL_HALO, POOL_W), lambda i, j: (i, 0, 0)), _full(w_pool, 2), _full(scale, 2)],
        out_widths=[POOL_W], extra_out_shapes=[], extra_out_specs=[],
        scratch=[pltpu.VMEM((bb, POOL_HALO + tt, POOL_W), F32)], prev_outs=prev)[0]


def _mlstm_kernel(q_ref, k_ref, v_ref, o_ref, g_ref, c0_ref, n0_ref, m0_ref, hng_ref, *rest, bb, l, nc, n_prev):
    y_ref, c_out, n_out, m_out, cs, ns, ms = rest[n_prev:]
    c = pl.program_id(1)

    @pl.when(c == 0)
    def _():
        cs[...] = c0_ref[...]
        ns[...] = n0_ref[...]
        ms[...] = m0_ref[...]

    row = lax.broadcasted_iota(I32, (l, l), 0)
    col = lax.broadcasted_iota(I32, (l, l), 1)
    causal = row >= col
    scale = MLSTM_DH ** -0.5
    padr = LANES - l if l < 16 else 0

    assert nc == 1
    tril = causal.astype(BF16)
    lane = lax.broadcasted_iota(I32, (l, LANES), 1)
    gate_cols, gate_rows = {}, {}

    def lockstep(chains):
        live = list(chains)
        while live:
            still = []
            for ch in live:
                try:
                    next(ch)
                    still.append(ch)
                except StopIteration:
                    pass
            live = still

    def gate_chain(b):
        gates = g_ref[b * l:(b + 1) * l, :]
        bcum = _dot_exact_lhs(tril, _log_sigmoid(gates))
        yield
        cols = jnp.where(lane < MLSTM_HEADS, gates, bcum)
        gate_cols[b] = cols
        gate_rows[b] = _rows_of(cols)
        yield

    def head_chain(b, h):
        rows_b = pl.ds(b * l, l)
        hl = slice(h * MLSTM_DH, (h + 1) * MLSTM_DH)
        cols, rows = gate_cols[b], gate_rows[b]
        i_col = cols[:, h:h + 1]
        b_col = cols[:, MLSTM_HEADS + h:MLSTM_HEADS + h + 1]
        i_row = rows[h:h + 1, :]
        b_row = rows[MLSTM_HEADS + h:MLSTM_HEADS + h + 1, :]
        c_prev, n_prev_, m_prev = cs[b, h], ns[b, h], ms[b, h][:, 0:1]
        qh = q_ref[rows_b, hl] * scale
        kh = k_ref[rows_b, hl]
        vh = v_ref[rows_b, hl]

        dmat = jnp.where(causal, b_col - b_row + i_row, -jnp.inf)
        inter = b_col + m_prev
        qk = lax.dot_general(qh, kh, (((1,), (1,)), ((), ())), preferred_element_type=F32)
        q_c = _dot(qh, c_prev)
        m_t = jnp.maximum(inter, jnp.max(dmat, axis=-1, keepdims=True))
        yield
        s = qk * jnp.exp(dmat - m_t)
        w_inter = jnp.exp(inter - m_t)
        num = _dot(s, vh) + w_inter * q_c
        den = jnp.sum(s, -1, keepdims=True) + w_inter * jnp.sum(qh * n_prev_, -1, keepdims=True)
        yield
        bound = jnp.maximum(jnp.abs(den), jnp.exp(-m_t))
        hh = num / bound
        mu = jnp.mean(hh, -1, keepdims=True)
        yield
        hc = hh - mu
        var = jnp.mean(hc * hc, -1, keepdims=True)
        yield
        hn = hc * lax.rsqrt(var + LN_EPS) * hng_ref[:, hl]
        y_ref[rows_b, hl] = jax.nn.sigmoid(o_ref[rows_b, hl]) * hn

        b_end = b_col[l - 1:l, :]
        gcol = b_end - b_col + i_col
        m_new = jnp.maximum(b_end + m_prev, jnp.max(gcol, axis=0, keepdims=True))
        kw = kh * jnp.exp(gcol - m_new)
        decay = jnp.exp(b_end + m_prev - m_new)
        if padr:
            zpad = jnp.zeros((padr, MLSTM_DH), F32)
            kv = lax.dot_general(jnp.concatenate([kw, zpad], 0), jnp.concatenate([vh, zpad], 0),
                                 (((0,), (0,)), ((), ())), preferred_element_type=F32)
        else:
            kv = lax.dot_general(kw, vh, (((0,), (0,)), ((), ())), preferred_element_type=F32)
        yield
        cs[b, h] = decay * c_prev + kv
        ns[b, h] = decay * n_prev_ + jnp.sum(kw, axis=0, keepdims=True)
        ms[b, h] = jnp.broadcast_to(m_new, (1, LANES))

    lockstep(gate_chain(b) for b in range(bb))
    for b0 in range(0, bb, MLSTM_LOCKSTEP_SEQS):
        lockstep(head_chain(b, h) for b in range(b0, min(bb, b0 + MLSTM_LOCKSTEP_SEQS)) for h in range(MLSTM_HEADS))

    @pl.when(c == pl.num_programs(1) - 1)
    def _():
        c_out[...] = cs[...]
        n_out[...] = ns[...]
        m_out[...] = ms[...]


def _mlstm(q, k, v, o, g, c0, n0, m0, hn_g, *, row0, b, t, bb, l, nc):
    st_c = pl.BlockSpec((bb, MLSTM_HEADS, MLSTM_DH, MLSTM_DH), lambda i, j: (i, 0, 0, 0))
    st_v = pl.BlockSpec((bb, MLSTM_HEADS, 1, LANES), lambda i, j: (i, 0, 0, 0))
    return _seq_call(
        functools.partial(_mlstm_kernel, bb=bb, l=l, nc=nc), name="mlstm",
        row0=row0, b=b, t=t, bb=bb, tt=nc * l, n_rows=q.shape[0],
        seq_ins=[q, k, v, o, g], seq_widths=[MLSTM_W] * 4 + [LANES],
        other_ins=[c0, n0, m0, hn_g], other_specs=[st_c, st_v, st_v, _full(hn_g, 2)],
        out_widths=[MLSTM_W],
        extra_out_shapes=[jax.ShapeDtypeStruct((b, MLSTM_HEADS, MLSTM_DH, MLSTM_DH), F32),
                          jax.ShapeDtypeStruct((b, MLSTM_HEADS, 1, LANES), F32),
                          jax.ShapeDtypeStruct((b, MLSTM_HEADS, 1, LANES), F32)],
        extra_out_specs=[st_c, st_v, st_v],
        scratch=[pltpu.VMEM((bb, MLSTM_HEADS, MLSTM_DH, MLSTM_DH), F32),
                 pltpu.VMEM((bb, MLSTM_HEADS, 1, LANES), F32),
                 pltpu.VMEM((bb, MLSTM_HEADS, 1, LANES), F32)],
        prev_outs=[], in_place=3)


def _mix_out_kernel(*refs, n_parts, n_x, tiles_p):
    parts = refs[:n_parts]
    ws = refs[n_parts:2 * n_parts]
    x_refs = refs[2 * n_parts:2 * n_parts + n_x]
    (bo_ref, g_ref, b_ref, wr_ref, br_ref,
     x1_ref, x1t_ref, route_ref, gate_ref, cnt_ref, carry) = refs[2 * n_parts + n_x:]
    i = pl.program_id(0)

    @pl.when(i == 0)
    def _():
        carry[...] = jnp.zeros_like(carry)

    mix = bo_ref[...]
    for a_ref, w_ref in zip(parts, ws):
        mix = mix + _dot(a_ref[...].astype(BF16), w_ref[...])
    x1 = _layer_norm(DN_ALPHA * _token_rows(x_refs, tiles_p) + mix, g_ref[...], b_ref[...])
    x1_ref[...] = x1
    tm = x1.shape[0]
    _store_token_tiles(x1t_ref, x1)

    logits = _dot(x1.astype(BF16), wr_ref[...]) + br_ref[...]
    lane = lax.broadcasted_iota(I32, (tm, LANES), 1)
    lane_f = lane.astype(F32)
    work = logits
    picks, vals = [], []
    onehot = jnp.zeros((tm, LANES), F32)
    for _ in range(TOP_K):
        mx = jnp.max(work, axis=-1, keepdims=True)
        pick = jnp.min(jnp.where(work == mx, lane_f, float(LANES)), axis=-1, keepdims=True)
        hit = lane_f == pick
        onehot = jnp.where(hit, 1.0, onehot)
        work = jnp.where(hit, -jnp.inf, work)
        picks.append(pick)
        vals.append(mx)
    exps = [jnp.exp(v - vals[0]) for v in vals]
    tot = exps[0] + exps[1] + exps[2] + exps[3]

    rr = lax.broadcasted_iota(I32, (tm, tm), 0)
    cc = lax.broadcasted_iota(I32, (tm, tm), 1)
    before = _dot((rr > cc).astype(BF16), onehot.astype(BF16)) + carry[...]
    route = jnp.zeros((tm, LANES), F32)
    gate_o = jnp.zeros((tm, LANES), F32)
    for kk in range(TOP_K):
        rk = jnp.sum(jnp.where(lane_f == picks[kk], before, 0.0), axis=-1, keepdims=True)
        route = jnp.where(lane == kk, picks[kk], route)
        route = jnp.where(lane == TOP_K + kk, rk, route)
        gate_o = jnp.where(lane == kk, exps[kk] / tot, gate_o)
    route_ref[...] = _rows_of(route).astype(I32)
    gate_ref[...] = gate_o
    carry[...] = carry[...] + jnp.sum(onehot, axis=0, keepdims=True)
    cnt_ref[...] = carry[...]


def _mix_out(parts, ws, x_srcs, b_out, ln_g, ln_b, w_r, b_r):
    tm = TOKEN_TILE
    x_specs, tiles_p, n = _token_specs(x_srcs)
    row = lambda c: pl.BlockSpec((tm, c), lambda i: (i, 0))
    small = [b_out, ln_g, ln_b, w_r, b_r]
    return pl.pallas_call(
        functools.partial(_mix_out_kernel, n_parts=len(parts), n_x=len(x_srcs), tiles_p=tiles_p),
        grid=(n // tm,),
        in_specs=[row(p.shape[1]) for p in parts] + [_full(w, 1) for w in ws] + x_specs
        + [_full(a, 1) for a in small],
        out_specs=[row(D_MODEL), pl.BlockSpec((tm * TILE_ROWS, LANES), lambda i: (i, 0)),
                   pl.BlockSpec((SUBLANES, tm), lambda i: (0, i)), row(LANES),
                   pl.BlockSpec((1, LANES), lambda i: (0, 0))],
        out_shape=[
            jax.ShapeDtypeStruct((n, D_MODEL), F32),
            jax.ShapeDtypeStruct((n * TILE_ROWS, LANES), F32),
            jax.ShapeDtypeStruct((SUBLANES, n), I32),
            jax.ShapeDtypeStruct((n, LANES), F32),
            jax.ShapeDtypeStruct((1, LANES), F32),
        ],
        scratch_shapes=[pltpu.VMEM((1, LANES), F32)],
        compiler_params=_cparams(("arbitrary",)),
        name="mix_out_router",
    )(*parts, *ws, *x_srcs, *small)


MOE_CHUNKS = 4
MOE_TAB_RING = 4
MOE_PAD_TILES = 2


def _moe_kernel(w_tile, w_exp, w_lo, w_hi, w_first, w_last, w_newexp, w_valid,
                tab_hbm, x_hbm, wgu_ref, bgu_ref, wdn_ref, bdn_ref, yk_hbm,
                wgu_b, wdn_b, h_scr, xbuf, ybuf, tab, tsem, gsem, ssem, *, n_tiles, tm):
    j = pl.program_id(0)
    t = w_tile[j]
    b2 = t & 1
    unroll = 8

    def tab_slot(tile):
        return (tile + MOE_PAD_TILES) & (MOE_TAB_RING - 1)

    def tab_copy(tile):
        return pltpu.make_async_copy(tab_hbm.at[pl.ds((tile + MOE_PAD_TILES) * (2 * tm), 2 * tm)],
                                     tab.at[pl.ds(tab_slot(tile) * (2 * tm), 2 * tm)], tsem.at[tab_slot(tile)])

    def token(ref, tok):
        return ref.at[pl.ds(pl.multiple_of(tok * TILE_ROWS, TILE_ROWS), TILE_ROWS), :]

    def gather_copy(tile, r):
        src = tab[tab_slot(tile) * (2 * tm) + r]
        return pltpu.make_async_copy(token(x_hbm, src), token(xbuf.at[tile & 1], r), gsem.at[tile & 1])

    def scatter_copy(tile, r):
        dst = tab[tab_slot(tile) * (2 * tm) + tm + r]
        return pltpu.make_async_copy(token(ybuf.at[tile & 1], r), token(yk_hbm, dst), ssem.at[tile & 1])

    def start_all(copy, tile):
        def body(i, carry):
            for u in range(unroll):
                copy(tile, i * unroll + u).start()
            return carry

        lax.fori_loop(0, tm // unroll, body, 0)

    def wait_gather(tile):
        pltpu.make_async_copy(x_hbm.at[pl.ds(0, tm * TILE_ROWS), :], xbuf.at[tile & 1], gsem.at[tile & 1]).wait()

    def wait_scatter(tile):
        pltpu.make_async_copy(ybuf.at[tile & 1], yk_hbm.at[pl.ds(0, tm * TILE_ROWS), :], ssem.at[tile & 1]).wait()

    @pl.when(j == 0)
    def _():
        for tile in (-2, -1, 0):
            cp = tab_copy(tile)
            cp.start()
            cp.wait()
        tab_copy(1).start()
        ybuf[...] = jnp.zeros(ybuf.shape, F32)
        start_all(scatter_copy, -2)
        start_all(gather_copy, 0)

    @pl.when(w_newexp[j] == 1)
    def _():
        wgu_b[...] = wgu_ref[0, 0].astype(BF16)
        wdn_b[...] = wdn_ref[0, 0].astype(BF16)

    chunk = D_FF // MOE_CHUNKS
    rows_per_chunk = tm // MOE_CHUNKS

    def experts(first_item):
        if first_item:
            wait_gather(t)
            tab_copy(t + 1).wait()
            tab_copy(t + 2).start()
        x = _load_token_tiles(xbuf.at[b2], tm, BF16)
        bgu = bgu_ref[0, 0]
        for c in range(MOE_CHUNKS):
            cg = slice(c * chunk, (c + 1) * chunk)
            cu = slice(D_FF + c * chunk, D_FF + (c + 1) * chunk)
            g = jnp.minimum(_dot(x, wgu_b[:, cg]) + bgu[:, cg], SWIGLU_LIMIT)
            u = jnp.clip(_dot(x, wgu_b[:, cu]) + bgu[:, cu], -SWIGLU_LIMIT, SWIGLU_LIMIT)
            h_scr[:, cg] = ((u + 1.0) * g * jax.nn.sigmoid(SWIGLU_ALPHA * g)).astype(BF16)
            if first_item:
                for r in range(c * rows_per_chunk, (c + 1) * rows_per_chunk):
                    gather_copy(t + 1, r).start()
        if first_item:
            wait_scatter(t - 2)
        else:
            row = lax.broadcasted_iota(I32, (tm, 1), 0)
            mine = (row >= w_lo[j]) & (row < w_hi[j])
        h = h_scr[...]
        bdn = bdn_ref[0, 0]
        out = ybuf.at[b2]
        for c in range(MOE_CHUNKS):
            cy = slice(c * chunk, (c + 1) * chunk)
            y = _dot(h, wdn_b[:, cy]) + bdn[:, cy]
            for s in range(chunk // LANES):
                piece = y[:, s * LANES:(s + 1) * LANES]
                rows = pl.ds(c * (chunk // LANES) + s, tm, stride=TILE_ROWS)
                out[rows, :] = piece if first_item else jnp.where(mine, piece, out[rows, :])
            if first_item:
                for r in range(c * rows_per_chunk, (c + 1) * rows_per_chunk):
                    scatter_copy(t - 1, r).start()

    @pl.when(w_first[j] == 1)
    def _():
        experts(True)

    @pl.when((w_valid[j] == 1) & (w_first[j] == 0))
    def _():
        experts(False)

    @pl.when((w_last[j] == 1) & (t == n_tiles - 1))
    def _():
        start_all(scatter_copy, t)
        wait_scatter(t - 1)
        wait_scatter(t)
        wait_gather(t + 1)
        tab_copy(t + 2).wait()


def _moe_experts(l, work, tab, x1t, w_gu, b_gu, w_dn, b_dn):
    n = x1t.shape[0] // TILE_ROWS
    tm = MOE_TILE
    n_tiles = n * TOP_K // tm
    n_work = work[0].shape[0]
    by_expert = lambda j, wt, we, *_: (l, we[j], 0, 0)
    grid_spec = pltpu.PrefetchScalarGridSpec(
        num_scalar_prefetch=8,
        grid=(n_work,),
        in_specs=[
            _ANY, _ANY,
            pl.BlockSpec((1, 1, D_MODEL, 2 * D_FF), by_expert),
            pl.BlockSpec((1, 1, 1, 2 * D_FF), by_expert),
            pl.BlockSpec((1, 1, D_FF, D_MODEL), by_expert),
            pl.BlockSpec((1, 1, 1, D_MODEL), by_expert),
        ],
        out_specs=_ANY,
        scratch_shapes=[
            pltpu.VMEM((D_MODEL, 2 * D_FF), BF16), pltpu.VMEM((D_FF, D_MODEL), BF16),
            pltpu.VMEM((tm, D_FF), BF16),
            pltpu.VMEM((2, tm * TILE_ROWS, LANES), F32),
            pltpu.VMEM((2, tm * TILE_ROWS, LANES), F32),
            pltpu.SMEM((MOE_TAB_RING * 2 * tm,), I32),
            pltpu.SemaphoreType.DMA((MOE_TAB_RING,)), pltpu.SemaphoreType.DMA((2,)), pltpu.SemaphoreType.DMA((2,)),
        ],
    )
    return pl.pallas_call(
        functools.partial(_moe_kernel, n_tiles=n_tiles, tm=tm),
        grid_spec=grid_spec,
        out_shape=jax.ShapeDtypeStruct(((TOP_K * n + MOE_PAD_TILES * tm) * TILE_ROWS, LANES), F32),
        compiler_params=_cparams(("arbitrary",)),
        name="moe_experts",
    )(*work, tab, x1t, w_gu, b_gu[:, :, None, :], w_dn, b_dn[:, :, None, :])


def _moe_work_items(counts, n_slots):
    tm = MOE_TILE
    n_tiles = n_slots // tm
    n_work = n_tiles + N_EXPERTS
    ends = jnp.cumsum(counts)
    offs = ends - counts
    cand = jnp.concatenate([jnp.arange(n_tiles, dtype=I32) * tm, offs])
    ci = jnp.arange(n_work, dtype=I32)
    less = (cand[None, :] < cand[:, None]) | ((cand[None, :] == cand[:, None]) & (ci[None, :] < ci[:, None]))
    order = jnp.sum(less.astype(I32), axis=1)
    starts = jnp.sum(jnp.where(order[None, :] == ci[:, None], cand[None, :], 0), axis=1)
    stops = jnp.concatenate([starts[1:], jnp.array([n_slots], I32)])
    valid = stops > starts
    vpos = jnp.cumsum(valid.astype(I32)) - 1
    n_valid = jnp.sum(valid.astype(I32))
    want = jnp.minimum(ci, n_valid - 1)
    pick = valid[None, :] & (vpos[None, :] == want[:, None])
    starts = jnp.sum(jnp.where(pick, starts[None, :], 0), axis=1)
    stops = jnp.sum(jnp.where(pick, stops[None, :], 0), axis=1)
    is_valid = ci < n_valid
    tile = starts // tm
    expert = jnp.minimum(jnp.sum((ends[None, :] <= starts[:, None]).astype(I32), axis=1), N_EXPERTS - 1)
    prev = lambda a: jnp.concatenate([jnp.array([-1], I32), a[:-1]])
    nxt = lambda a: jnp.concatenate([a[1:], jnp.array([-1], I32)])
    first = (tile != prev(tile)) & is_valid
    last = ((tile != nxt(tile)) | (ci == n_valid - 1)) & is_valid
    newexp = (expert != prev(expert)) & is_valid
    as_i = lambda a: a.astype(I32)
    return offs, (tile, expert, starts - tile * tm, stops - tile * tm, as_i(first), as_i(last), as_i(newexp),
                  as_i(is_valid))


def _ffn_ln_kernel(x_ref, *rest, split_tile):
    yk_refs = rest[:TOP_K]
    gate_ref, g_ref, b_ref = rest[TOP_K:TOP_K + 3]
    out_refs = rest[TOP_K + 3:]
    tm = x_ref.shape[0]
    gate = gate_ref[...]
    ff = gate[:, 0:1] * _load_token_tiles(yk_refs[0], tm)
    for kk in range(1, TOP_K):
        ff = ff + gate[:, kk:kk + 1] * _load_token_tiles(yk_refs[kk], tm)
    out = _layer_norm(DN_ALPHA * x_ref[...] + ff, g_ref[...], b_ref[...])
    if split_tile is None:
        out_refs[0][...] = out
    else:
        @pl.when(pl.program_id(0) < split_tile)
        def _():
            out_refs[0][...] = out.reshape(out_refs[0].shape)

        @pl.when(pl.program_id(0) >= split_tile)
        def _():
            out_refs[1][...] = out


def _ffn_ln(x, yk, gate, ln_g, ln_b, split=None):
    n = x.shape[0]
    tm = TOKEN_TILE
    pick = lambda kk: pl.BlockSpec((tm * TILE_ROWS, LANES), lambda i: (kk * (n // tm) + i, 0))
    row = lambda c: pl.BlockSpec((tm, c), lambda i: (i, 0))
    vec = pl.BlockSpec((1, D_MODEL), lambda i: (0, 0))
    if split is None:
        split_tile = None
        out_specs = row(D_MODEL)
        out_shape = jax.ShapeDtypeStruct((n, D_MODEL), F32)
    else:
        b, t = split
        chunk = tm // b
        split_tile = t // chunk
        out_specs = [pl.BlockSpec((b, chunk, D_MODEL), lambda i: (0, jnp.minimum(i, split_tile - 1), 0)),
                     pl.BlockSpec((tm, D_MODEL), lambda i: (jnp.maximum(i - split_tile, 0), 0))]
        out_shape = [jax.ShapeDtypeStruct((b, t, D_MODEL), F32), jax.ShapeDtypeStruct((n - b * t, D_MODEL), F32)]
    return pl.pallas_call(
        functools.partial(_ffn_ln_kernel, split_tile=split_tile),
        grid=(n // tm,),
        in_specs=[row(D_MODEL)] + [pick(kk) for kk in range(TOP_K)] + [row(LANES), vec, vec],
        out_specs=out_specs,
        out_shape=out_shape,
        compiler_params=_cparams(("arbitrary",)),
        name="ffn_ln",
    )(x, *([yk] * TOP_K), gate, ln_g, ln_b)


def _moe_block(l, x1, x1t, route_t, gate, counts, ln_g, ln_b, w_gu, b_gu, w_dn, b_dn, split=None):
    n = x1.shape[0]
    n_slots = n * TOP_K
    tm = MOE_TILE
    offs, work = _moe_work_items(counts, n_slots)
    idx_t, rank_t = route_t[:TOP_K], route_t[TOP_K:]
    experts = jnp.arange(N_EXPERTS, dtype=I32)
    slot_t = rank_t + jnp.sum(jnp.where(idx_t[..., None] == experts, offs, 0), axis=-1)
    pair = jnp.argsort(slot_t.reshape(-1)).astype(I32)
    spare = n_slots + jnp.arange(MOE_PAD_TILES * tm, dtype=I32).reshape(MOE_PAD_TILES, tm)
    first = jnp.arange(MOE_PAD_TILES * tm, dtype=I32).reshape(MOE_PAD_TILES, tm) % tm
    src = jnp.concatenate([first, (pair % n).reshape(-1, tm), first], axis=0)
    dst = jnp.concatenate([spare, pair.reshape(-1, tm), spare], axis=0)
    tab = jnp.concatenate([src, dst], axis=1).reshape(-1)
    yk = _moe_experts(l, work, tab, x1t, w_gu, b_gu, w_dn, b_dn)
    return _ffn_ln(x1, yk, gate, ln_g[l][None], ln_b[l][None], split)


def _cv_in_kernel(x_ref, w_ref, b_ref, g_ref):
    xb = x_ref[...].astype(BF16)
    a = _dot(xb, w_ref[:, :CONV_W]) + b_ref[:, :CONV_W]
    gate = _dot(xb, w_ref[:, CONV_W:]) + b_ref[:, CONV_W:]
    _store_token_tiles(g_ref, a * jax.nn.sigmoid(gate))


def _cv_in_proj(x, w, b):
    n = x.shape[0]
    tm = TOKEN_TILE
    return pl.pallas_call(
        _cv_in_kernel,
        grid=(n // tm,),
        in_specs=[pl.BlockSpec((tm, D_MODEL), lambda i: (i, 0)), _full(w, 1), _full(b, 1)],
        out_specs=pl.BlockSpec((tm * TILE_ROWS, LANES), lambda i: (i, 0)),
        out_shape=jax.ShapeDtypeStruct((n * TILE_ROWS, LANES), F32),
        compiler_params=_cparams(("parallel",)),
        name="cv_in_proj",
    )(x, w, b)


CONV_GROUP = 16


def _dwconv_kernel(g_ref, past_ref, w_ref, wb_ref, lg_ref, lb_ref, *rest, bb, tt, n_prev):
    c_ref, ext, conv = rest[n_prev:]
    j = pl.program_id(1)
    halo = CONV_HALO * TILE_ROWS

    @pl.when(j == 0)
    def _():
        for b in range(bb):
            history = past_ref[b]
            for s in range(TILE_ROWS):
                rows = pl.ds((CONV_HALO - CONV_STATE) * TILE_ROWS + s, CONV_STATE, stride=TILE_ROWS)
                ext.at[b][rows, :] = history[:, s * LANES:(s + 1) * LANES]

    ext[:, halo:, :] = g_ref[...].reshape(bb, tt * TILE_ROWS, LANES)
    group = min(CONV_GROUP, tt)
    first_tap = CONV_HALO - CONV_STATE

    for b in range(bb):
        def per_group(gi, carry):
            t0 = gi * group
            acc = None
            for d in range(CONV_K):
                rows = pl.ds(pl.multiple_of((first_tap + d + t0) * TILE_ROWS, TILE_ROWS), group * TILE_ROWS)
                window = ext[b, rows, :].reshape(group, TILE_ROWS, LANES)
                term = window * w_ref[d * TILE_ROWS:(d + 1) * TILE_ROWS, :][None]
                acc = term if acc is None else acc + term
            out_rows = pl.ds(pl.multiple_of((b * tt + t0) * TILE_ROWS, TILE_ROWS), group * TILE_ROWS)
            conv[out_rows, :] = acc.reshape(group * TILE_ROWS, LANES)
            return carry

        lax.fori_loop(0, tt // group, per_group, 0)

    c = _layer_norm(_load_token_tiles(conv, bb * tt) + wb_ref[...], lg_ref[...], lb_ref[...])
    c_ref[...] = c * jax.nn.sigmoid(c)
    ext[:, 0:halo, :] = ext[:, tt * TILE_ROWS:tt * TILE_ROWS + halo, :]


def _dwconv(g, past, dw_w, dw_b, ln_g, ln_b, *, row0, b, t, bb, tt, prev):
    vec = pl.BlockSpec((1, CONV_W), lambda i, j: (0, 0))
    return _seq_call(
        functools.partial(_dwconv_kernel, bb=bb, tt=tt), name="dwconv_ln_swish",
        row0=row0, b=b, t=t, bb=bb, tt=tt, n_rows=g.shape[0] // TILE_ROWS,
        seq_ins=[g], seq_widths=[None],
        other_ins=[past, dw_w, dw_b, ln_g, ln_b],
        other_specs=[pl.BlockSpec((bb, CONV_STATE, CONV_W), lambda i, j: (i, 0, 0)), _full(dw_w, 2),
                     vec, vec, vec],
        out_widths=[CONV_W], extra_out_shapes=[], extra_out_specs=[],
        scratch=[pltpu.VMEM((bb, (CONV_HALO + tt) * TILE_ROWS, LANES), F32),
                 pltpu.VMEM((bb * tt * TILE_ROWS, LANES), F32)],
        prev_outs=prev)[0]


def _route_args(w_router, b_router, l):
    w_r = jnp.pad(w_router[l], ((0, 0), (0, LANES - N_EXPERTS))).astype(BF16)
    b_r = jnp.pad(b_router[l], (0, LANES - N_EXPERTS), constant_values=NEG_BIG)[None]
    return w_r, b_r


def kernel(x_prompt, x_sample, state_pool, state_mlstm_C, state_mlstm_n, state_mlstm_m, state_conv, w_in_pm, b_if, w_pool, pool_scale, hn_g, w_out_pm, b_out_pm, w_in_cv, b_in_cv, dw_w, dw_b, cln_g, cln_b, w_out_cv, b_out_cv, ln_mix_g, ln_mix_b, ln_ffn_g, ln_ffn_b, w_router, b_router, w_gu, b_gu, w_dn, b_dn):
    bp, tp, _ = x_prompt.shape
    bs, ts, _ = x_sample.shape
    n_p = bp * tp
    n = n_p + bs * ts
    x_srcs = [x_prompt, x_sample.reshape(bs * ts, D_MODEL)]
    chunk = TOKEN_TILE // bp
    assert chunk == MLSTM_CHUNK and tp % chunk == 0 and chunk >= max(POOL_STATE, CONV_STATE)

    groups = (
        dict(row0=0, b=bp, t=tp, pos0=0, pool_bb=bp, pool_tt=chunk, l=chunk, mlstm_bb=bp, mlstm_nc=1, conv_bb=bp,
             conv_tt=chunk, chunk_major=True,
             past_pool=jnp.zeros((bp, POOL_STATE, POOL_W), F32),
             c0=jnp.zeros((bp, MLSTM_HEADS, MLSTM_DH, MLSTM_DH), F32),
             n0=jnp.zeros((bp, MLSTM_HEADS, MLSTM_DH), F32),
             m0=jnp.zeros((bp, MLSTM_HEADS), F32),
             past_conv=jnp.zeros((bp, CONV_STATE, CONV_W), F32)),
        dict(row0=n_p, b=bs, t=ts, pos0=PAST_LEN, pool_bb=16, pool_tt=ts, l=ts, mlstm_bb=8, mlstm_nc=1, conv_bb=8,
             conv_tt=ts, chunk_major=False,
             past_pool=state_pool[0], c0=state_mlstm_C[0], n0=state_mlstm_n[0], m0=state_mlstm_m[0],
             past_conv=state_conv[0]),
    )

    def tail_state(past, a, gr, keep, tiled=False):
        b, t, row0 = gr["b"], gr["t"], gr["row0"]
        rpt = TILE_ROWS if tiled else 1
        width = a.shape[-1] * rpt
        if gr["chunk_major"]:
            ends = [row0 + (t - chunk) * b + (i + 1) * chunk for i in range(b)]
            parts = [a[(e - keep) * rpt:e * rpt] for e in ends]
            return jnp.stack(parts).reshape(b, keep, width)
        new = a[row0 * rpt:(row0 + b * t) * rpt].reshape(b, t, width)
        return jnp.concatenate([past[:, t:], new], axis=1)

    w_in = w_in_pm[0]
    w_main = w_in[:, :POOL_W + 4 * MLSTM_W].astype(BF16)
    n_gate = 2 * MLSTM_HEADS
    w_gate = jnp.pad(w_in[:, POOL_W + 4 * MLSTM_W:], ((0, 0), (0, LANES - n_gate))).astype(BF16)
    b_gate = jnp.pad(b_if[0], (0, LANES - n_gate))[None]
    w_out = w_out_pm[0].astype(BF16)
    w_r, b_r = _route_args(w_router, b_router, 0)

    u, q, k, v, o, g = _pm_in_proj(x_srcs, w_main, w_gate, b_gate)
    y_m, states = o, []
    for gr in groups:
        b, t = gr["b"], gr["t"]
        n0 = gr["n0"][:, :, None, :]
        m0 = jnp.broadcast_to(gr["m0"][:, :, None, None], (b, MLSTM_HEADS, 1, LANES))
        y_m, c_new, n_new, m_new = _mlstm(q, k, v, y_m, g, gr["c0"], n0, m0, hn_g[0][None], row0=gr["row0"], b=b,
                                          t=t, bb=gr["mlstm_bb"], l=gr["l"], nc=gr["mlstm_nc"])
        states.append((tail_state(gr["past_pool"], u, gr, POOL_STATE), c_new, n_new[:, :, 0, :], m_new[:, :, 0, 0]))
    y_pool = q
    for gr in groups:
        past = jnp.pad(gr["past_pool"], ((0, 0), (POOL_HALO - POOL_STATE, 0), (0, 0)))
        y_pool = _pool_mix(u, past, w_pool[0].astype(BF16), pool_scale[0][None], row0=gr["row0"], b=gr["b"],
                           t=gr["t"], bb=gr["pool_bb"], tt=gr["pool_tt"], pos0=gr["pos0"], prev=[y_pool])
    x1, x1t, route_t, gate, counts = _mix_out([y_pool, y_m], [w_out[:POOL_W], w_out[POOL_W:]], x_srcs,
                                              b_out_pm[0][None], ln_mix_g[0][None], ln_mix_b[0][None], w_r, b_r)
    x2 = _moe_block(0, x1, x1t, route_t, gate, counts[0, :N_EXPERTS].astype(I32), ln_ffn_g, ln_ffn_b,
                    w_gu, b_gu, w_dn, b_dn)

    w_r, b_r = _route_args(w_router, b_router, 1)
    dw_w_p = jnp.pad(dw_w[0], ((0, CONV_HALO - CONV_K), (0, 0))).reshape(CONV_HALO * TILE_ROWS, LANES)
    gl = _cv_in_proj(x2, w_in_cv[0].astype(BF16), b_in_cv[0][None])
    c, conv_states = x1, []
    for gr in groups:
        c = _dwconv(gl, gr["past_conv"], dw_w_p, dw_b[0][None], cln_g[0][None], cln_b[0][None], row0=gr["row0"], b=gr["b"],
                    t=gr["t"], bb=gr["conv_bb"], tt=gr["conv_tt"], prev=[c])
        conv_states.append(tail_state(gr["past_conv"], gl, gr, CONV_STATE, tiled=True))
    x3, x3t, route_t, gate, counts = _mix_out([c], [w_out_cv[0].astype(BF16)], [x2], b_out_cv[0][None],
                                              ln_mix_g[1][None], ln_mix_b[1][None], w_r, b_r)
    y_p, y_s = _moe_block(1, x3, x3t, route_t, gate, counts[0, :N_EXPERTS].astype(I32), ln_ffn_g, ln_ffn_b,
                          w_gu, b_gu, w_dn, b_dn, split=(bp, tp))

    (pool_p, c_p, n_p_, m_p), (pool_s, c_s, n_s, m_s) = states
    return (y_p, y_s.reshape(bs, ts, D_MODEL),
            pool_p[None], pool_s[None], c_p[None], c_s[None], n_p_[None], n_s[None], m_p[None], m_s[None],
            conv_states[0][None], conv_states[1][None])
```

```python
import functools

import jax
import jax.numpy as jnp
from jax import lax
from jax.experimental import pallas as pl
from jax.experimental.pallas import tpu as pltpu

F32 = jnp.float32
BF16 = jnp.bfloat16
I32 = jnp.int32

D_MODEL = 1024
DEPTH = 2
POOL_W = 512
POOL_WINDOWS = (2, 4, 8, 16)
POOL_GW = 128
POOL_STATE = 15
POOL_HALO = 16
MLSTM_HEADS = 4
MLSTM_W = 512
MLSTM_DH = 128
MLSTM_CHUNK = 64
MLSTM_LOCKSTEP_SEQS = 2
CONV_W = 1024
CONV_K = 31
CONV_STATE = 30
CONV_HALO = 32
N_EXPERTS = 32
TOP_K = 4
D_FF = 1024
SWIGLU_LIMIT = 7.0
SWIGLU_ALPHA = 1.702
LN_EPS = 1e-5
DN_ALPHA = (2 * DEPTH) ** 0.25
PAST_LEN = 16384

LANES = 128
SUBLANES = 8
TOKEN_TILE = 512
MOE_TILE = 512
NEG_BIG = -1e30
VMEM_LIMIT = 56 * 1024 * 1024


def _cparams(sem, vmem=VMEM_LIMIT):
    return pltpu.CompilerParams(dimension_semantics=sem, vmem_limit_bytes=vmem)


def _layer_norm(x, g, b):
    mu = jnp.mean(x, -1, keepdims=True)
    xc = x - mu
    var = jnp.mean(xc * xc, -1, keepdims=True)
    return xc * lax.rsqrt(var + LN_EPS) * g + b


def _log_sigmoid(x):
    return jnp.minimum(x, 0.0) - jnp.log1p(jnp.exp(-jnp.abs(x)))


def _dot(a, b):
    return jnp.dot(a, b, preferred_element_type=F32)


def _bf16_pieces(x):
    hi = x.astype(BF16)
    rest = x - hi.astype(F32)
    mid = rest.astype(BF16)
    return hi, mid, (rest - mid.astype(F32)).astype(BF16)


def _dot_exact_lhs(a, x, dims=(((1,), (0,)), ((), ()))):
    a = a.astype(BF16)
    return sum(lax.dot_general(a, p, dims, preferred_element_type=F32) for p in _bf16_pieces(x))


def _rows_of(cols):
    sel = (lax.broadcasted_iota(I32, (SUBLANES, LANES), 0) == lax.broadcasted_iota(I32, (SUBLANES, LANES), 1))
    return _dot_exact_lhs(sel, cols, (((1,), (1,)), ((), ())))


TILE_ROWS = D_MODEL // LANES


def _store_token_tiles(ref, val):
    tokens = val.shape[0]
    for s in range(TILE_ROWS):
        ref[pl.ds(s, tokens, stride=TILE_ROWS), :] = val[:, s * LANES:(s + 1) * LANES]


def _load_token_tiles(ref, tokens, dtype=F32):
    return jnp.concatenate([ref[pl.ds(s, tokens, stride=TILE_ROWS), :].astype(dtype) for s in range(TILE_ROWS)],
                           axis=1)


_ANY = pl.BlockSpec(memory_space=pl.ANY)


def _full(a, grid_rank):
    return pl.BlockSpec(a.shape, lambda *_: (0,) * a.ndim)


def _token_specs(x_srcs):
    tm = TOKEN_TILE
    if len(x_srcs) == 1:
        return [pl.BlockSpec((tm, D_MODEL), lambda i: (i, 0))], None, x_srcs[0].shape[0]
    x_p, x_s = x_srcs
    b, t, _ = x_p.shape
    chunk = tm // b
    tiles_p = t // chunk
    assert b * chunk == tm and tiles_p * chunk == t and x_s.shape[0] % tm == 0
    specs = [pl.BlockSpec((b, chunk, D_MODEL), lambda i: (0, jnp.minimum(i, tiles_p - 1), 0)),
             pl.BlockSpec((tm, D_MODEL), lambda i: (jnp.maximum(i - tiles_p, 0), 0))]
    return specs, tiles_p, b * t + x_s.shape[0]


def _token_rows(x_refs, tiles_p):
    if len(x_refs) == 1:
        return x_refs[0][...]
    x_p, x_s = x_refs
    return jnp.where(pl.program_id(0) < tiles_p, x_p[...].reshape(TOKEN_TILE, D_MODEL), x_s[...])


def _pm_in_kernel(*refs, n_x, tiles_p):
    x_refs = refs[:n_x]
    w_ref, wg_ref, bg_ref, u_ref, q_ref, k_ref, v_ref, o_ref, g_ref = refs[n_x:]
    xb = _token_rows(x_refs, tiles_p).astype(BF16)
    for j, out in enumerate((u_ref, q_ref, k_ref, v_ref, o_ref)):
        out[...] = _dot(xb, w_ref[:, j * 512:(j + 1) * 512])
    g_ref[...] = _dot(xb, wg_ref[...]) + bg_ref[...]


def _pm_in_proj(x_srcs, w_main, w_gate, b_gate):
    tm = TOKEN_TILE
    x_specs, tiles_p, n = _token_specs(x_srcs)
    row = lambda c: pl.BlockSpec((tm, c), lambda i: (i, 0))
    return pl.pallas_call(
        functools.partial(_pm_in_kernel, n_x=len(x_srcs), tiles_p=tiles_p),
        grid=(n // tm,),
        in_specs=x_specs + [_full(w_main, 1), _full(w_gate, 1), _full(b_gate, 1)],
        out_specs=[row(512)] * 5 + [row(LANES)],
        out_shape=[jax.ShapeDtypeStruct((n, 512), F32)] * 5 + [jax.ShapeDtypeStruct((n, LANES), F32)],
        compiler_params=_cparams(("arbitrary",)),
        name="pm_in_proj",
    )(*x_srcs, w_main, w_gate, b_gate)


def _seq_call(kernel, *, name, row0, b, t, bb, tt, n_rows, seq_ins, seq_widths, other_ins, other_specs,
              out_widths, extra_out_shapes, extra_out_specs, scratch, prev_outs, in_place=None):
    steps = t // tt
    blk0 = row0 // (bb * tt)

    def seq_spec(c):
        shape = (bb * tt * TILE_ROWS, LANES) if c is None else (bb * tt, c)
        return pl.BlockSpec(shape, lambda i, j: (blk0 + i * steps + j, 0))

    prev = list(prev_outs)
    first_prev = len(seq_ins) + len(other_ins)
    aliases = {first_prev + o: o for o in range(len(prev))}
    if in_place is not None:
        aliases[in_place] = 0
    return pl.pallas_call(
        functools.partial(kernel, n_prev=len(prev)),
        grid=(b // bb, steps),
        in_specs=[seq_spec(c) for c in seq_widths] + list(other_specs) + [_ANY] * len(prev),
        out_specs=[seq_spec(c) for c in out_widths] + list(extra_out_specs),
        out_shape=[jax.ShapeDtypeStruct((n_rows, c), F32) for c in out_widths] + list(extra_out_shapes),
        input_output_aliases=aliases,
        scratch_shapes=scratch,
        compiler_params=_cparams(("parallel", "arbitrary")),
        name=name,
    )(*seq_ins, *other_ins, *prev)


def _pool_kernel(u_ref, past_ref, w_ref, sc_ref, *rest, bb, tt, pos0, n_prev):
    y_ref, ext = rest[n_prev:]
    j = pl.program_id(1)

    @pl.when(j == 0)
    def _():
        ext[:, 0:POOL_HALO, :] = past_ref[...]

    u = u_ref[...].reshape(bb, tt, POOL_W)
    ext[:, POOL_HALO:POOL_HALO + tt, :] = u
    pos = pos0 + j * tt + lax.broadcasted_iota(I32, (tt, 1), 0)
    for g, w in enumerate(POOL_WINDOWS):
        lanes = slice(g * POOL_GW, (g + 1) * POOL_GW)
        wsum = ext[:, POOL_HALO:POOL_HALO + tt, lanes]
        for d in range(1, w):
            wsum = wsum + ext[:, POOL_HALO - d:POOL_HALO - d + tt, lanes]
        cnt = jnp.minimum(w, pos + 1).astype(F32)
        pooled = wsum / cnt[None] - u[:, :, lanes]
        y = _dot(pooled.reshape(bb * tt, POOL_GW).astype(BF16), w_ref[g])
        y_ref[:, lanes] = y * sc_ref[:, lanes]
    ext[:, 0:POOL_HALO, :] = ext[:, tt:tt + POOL_HALO, :]


def _pool_mix(u, past, w_pool, scale, *, row0, b, t, bb, tt, pos0, prev):
    return _seq_call(
        functools.partial(_pool_kernel, bb=bb, tt=tt, pos0=pos0), name="pool_mix",
        row0=row0, b=b, t=t, bb=bb, tt=tt, n_rows=u.shape[0],
        seq_ins=[u], seq_widths=[POOL_W],
        other_ins=[past, w_pool, scale],
        other_specs=[pl.BlockSpec((bb, POOL_HALO, POOL_W), lambda i, j: (i, 0, 0)), _full(w_pool, 2), _full(scale, 2)],
        out_widths=[POOL_W], extra_out_shapes=[], extra_out_specs=[],
        scratch=[pltpu.VMEM((bb, POOL_HALO + tt, POOL_W), F32)], prev_outs=prev)[0]


def _mlstm_kernel(q_ref, k_ref, v_ref, o_ref, g_ref, c0_ref, n0_ref, m0_ref, hng_ref, *rest, bb, l, nc, n_prev):
    y_ref, c_out, n_out, m_out, cs, ns, ms = rest[n_prev:]
    c = pl.program_id(1)

    @pl.when(c == 0)
    def _():
        cs[...] = c0_ref[...]
        ns[...] = n0_ref[...]
        ms[...] = m0_ref[...]

    row = lax.broadcasted_iota(I32, (l, l), 0)
    col = lax.broadcasted_iota(I32, (l, l), 1)
    causal = row >= col
    scale = MLSTM_DH ** -0.5
    padr = LANES - l if l < 16 else 0

    assert nc == 1
    tril = causal.astype(BF16)
    lane = lax.broadcasted_iota(I32, (l, LANES), 1)
    gate_cols, gate_rows = {}, {}

    def lockstep(chains):
        live = list(chains)
        while live:
            still = []
            for ch in live:
                try:
                    next(ch)
                    still.append(ch)
                except StopIteration:
                    pass
            live = still

    def gate_chain(b):
        gates = g_ref[b * l:(b + 1) * l, :]
        bcum = _dot_exact_lhs(tril, _log_sigmoid(gates))
        yield
        cols = jnp.where(lane < MLSTM_HEADS, gates, bcum)
        gate_cols[b] = cols
        gate_rows[b] = _rows_of(cols)
        yield

    def head_chain(b, h):
        rows_b = pl.ds(b * l, l)
        hl = slice(h * MLSTM_DH, (h + 1) * MLSTM_DH)
        cols, rows = gate_cols[b], gate_rows[b]
        i_col = cols[:, h:h + 1]
        b_col = cols[:, MLSTM_HEADS + h:MLSTM_HEADS + h + 1]
        i_row = rows[h:h + 1, :]
        b_row = rows[MLSTM_HEADS + h:MLSTM_HEADS + h + 1, :]
        c_prev, n_prev_, m_prev = cs[b, h], ns[b, h], ms[b, h][:, 0:1]
        qh = q_ref[rows_b, hl] * scale
        kh = k_ref[rows_b, hl]
        vh = v_ref[rows_b, hl]

        dmat = jnp.where(causal, b_col - b_row + i_row, -jnp.inf)
        inter = b_col + m_prev
        qk = lax.dot_general(qh, kh, (((1,), (1,)), ((), ())), preferred_element_type=F32)
        q_c = _dot(qh, c_prev)
        m_t = jnp.maximum(inter, jnp.max(dmat, axis=-1, keepdims=True))
        yield
        s = qk * jnp.exp(dmat - m_t)
        w_inter = jnp.exp(inter - m_t)
        num = _dot(s, vh) + w_inter * q_c
        den = jnp.sum(s, -1, keepdims=True) + w_inter * jnp.sum(qh * n_prev_, -1, keepdims=True)
        yield
        bound = jnp.maximum(jnp.abs(den), jnp.exp(-m_t))
        hh = num / bound
        mu = jnp.mean(hh, -1, keepdims=True)
        yield
        hc = hh - mu
        var = jnp.mean(hc * hc, -1, keepdims=True)
        yield
        hn = hc * lax.rsqrt(var + LN_EPS) * hng_ref[:, hl]
        y_ref[rows_b, hl] = jax.nn.sigmoid(o_ref[rows_b, hl]) * hn

        b_end = b_col[l - 1:l, :]
        gcol = b_end - b_col + i_col
        m_new = jnp.maximum(b_end + m_prev, jnp.max(gcol, axis=0, keepdims=True))
        kw = kh * jnp.exp(gcol - m_new)
        decay = jnp.exp(b_end + m_prev - m_new)
        if padr:
            zpad = jnp.zeros((padr, MLSTM_DH), F32)
            kv = lax.dot_general(jnp.concatenate([kw, zpad], 0), jnp.concatenate([vh, zpad], 0),
                                 (((0,), (0,)), ((), ())), preferred_element_type=F32)
        else:
            kv = lax.dot_general(kw, vh, (((0,), (0,)), ((), ())), preferred_element_type=F32)
        yield
        cs[b, h] = decay * c_prev + kv
        ns[b, h] = decay * n_prev_ + jnp.sum(kw, axis=0, keepdims=True)
        ms[b, h] = jnp.broadcast_to(m_new, (1, LANES))

    lockstep(gate_chain(b) for b in range(bb))
    for b0 in range(0, bb, MLSTM_LOCKSTEP_SEQS):
        lockstep(head_chain(b, h) for b in range(b0, min(bb, b0 + MLSTM_LOCKSTEP_SEQS)) for h in range(MLSTM_HEADS))

    @pl.when(c == pl.num_programs(1) - 1)
    def _():
        c_out[...] = cs[...]
        n_out[...] = ns[...]
        m_out[...] = ms[...]


def _mlstm(q, k, v, o, g, c0, n0, m0, hn_g, *, row0, b, t, bb, l, nc):
    st_c = pl.BlockSpec((bb, MLSTM_HEADS, MLSTM_DH, MLSTM_DH), lambda i, j: (i, 0, 0, 0))
    st_v = pl.BlockSpec((bb, MLSTM_HEADS, 1, LANES), lambda i, j: (i, 0, 0, 0))
    return _seq_call(
        functools.partial(_mlstm_kernel, bb=bb, l=l, nc=nc), name="mlstm",
        row0=row0, b=b, t=t, bb=bb, tt=nc * l, n_rows=q.shape[0],
        seq_ins=[q, k, v, o, g], seq_widths=[MLSTM_W] * 4 + [LANES],
        other_ins=[c0, n0, m0, hn_g], other_specs=[st_c, st_v, st_v, _full(hn_g, 2)],
        out_widths=[MLSTM_W],
        extra_out_shapes=[jax.ShapeDtypeStruct((b, MLSTM_HEADS, MLSTM_DH, MLSTM_DH), F32),
                          jax.ShapeDtypeStruct((b, MLSTM_HEADS, 1, LANES), F32),
                          jax.ShapeDtypeStruct((b, MLSTM_HEADS, 1, LANES), F32)],
        extra_out_specs=[st_c, st_v, st_v],
        scratch=[pltpu.VMEM((bb, MLSTM_HEADS, MLSTM_DH, MLSTM_DH), F32),
                 pltpu.VMEM((bb, MLSTM_HEADS, 1, LANES), F32),
                 pltpu.VMEM((bb, MLSTM_HEADS, 1, LANES), F32)],
        prev_outs=[], in_place=3)


def _mix_out_kernel(*refs, n_parts, n_x, tiles_p):
    parts = refs[:n_parts]
    ws = refs[n_parts:2 * n_parts]
    x_refs = refs[2 * n_parts:2 * n_parts + n_x]
    (bo_ref, g_ref, b_ref, wr_ref, br_ref,
     x1_ref, x1t_ref, route_ref, gate_ref, cnt_ref, carry) = refs[2 * n_parts + n_x:]
    i = pl.program_id(0)

    @pl.when(i == 0)
    def _():
        carry[...] = jnp.zeros_like(carry)

    mix = bo_ref[...]
    for a_ref, w_ref in zip(parts, ws):
        mix = mix + _dot(a_ref[...].astype(BF16), w_ref[...])
    x1 = _layer_norm(DN_ALPHA * _token_rows(x_refs, tiles_p) + mix, g_ref[...], b_ref[...])
    x1_ref[...] = x1
    tm = x1.shape[0]
    _store_token_tiles(x1t_ref, x1)

    logits = _dot(x1.astype(BF16), wr_ref[...]) + br_ref[...]
    lane = lax.broadcasted_iota(I32, (tm, LANES), 1)
    lane_f = lane.astype(F32)
    work = logits
    picks, vals = [], []
    onehot = jnp.zeros((tm, LANES), F32)
    for _ in range(TOP_K):
        mx = jnp.max(work, axis=-1, keepdims=True)
        pick = jnp.min(jnp.where(work == mx, lane_f, float(LANES)), axis=-1, keepdims=True)
        hit = lane_f == pick
        onehot = jnp.where(hit, 1.0, onehot)
        work = jnp.where(hit, -jnp.inf, work)
        picks.append(pick)
        vals.append(mx)
    exps = [jnp.exp(v - vals[0]) for v in vals]
    tot = exps[0] + exps[1] + exps[2] + exps[3]

    rr = lax.broadcasted_iota(I32, (tm, tm), 0)
    cc = lax.broadcasted_iota(I32, (tm, tm), 1)
    before = _dot((rr > cc).astype(BF16), onehot.astype(BF16)) + carry[...]
    route = jnp.zeros((tm, LANES), F32)
    gate_o = jnp.zeros((tm, LANES), F32)
    for kk in range(TOP_K):
        rk = jnp.sum(jnp.where(lane_f == picks[kk], before, 0.0), axis=-1, keepdims=True)
        route = jnp.where(lane == kk, picks[kk], route)
        route = jnp.where(lane == TOP_K + kk, rk, route)
        gate_o = jnp.where(lane == kk, exps[kk] / tot, gate_o)
    route_ref[...] = _rows_of(route).astype(I32)
    gate_ref[...] = gate_o
    carry[...] = carry[...] + jnp.sum(onehot, axis=0, keepdims=True)
    cnt_ref[...] = carry[...]


def _mix_out(parts, ws, x_srcs, b_out, ln_g, ln_b, w_r, b_r):
    tm = TOKEN_TILE
    x_specs, tiles_p, n = _token_specs(x_srcs)
    row = lambda c: pl.BlockSpec((tm, c), lambda i: (i, 0))
    small = [b_out, ln_g, ln_b, w_r, b_r]
    return pl.pallas_call(
        functools.partial(_mix_out_kernel, n_parts=len(parts), n_x=len(x_srcs), tiles_p=tiles_p),
        grid=(n // tm,),
        in_specs=[row(p.shape[1]) for p in parts] + [_full(w, 1) for w in ws] + x_specs
        + [_full(a, 1) for a in small],
        out_specs=[row(D_MODEL), pl.BlockSpec((tm * TILE_ROWS, LANES), lambda i: (i, 0)),
                   pl.BlockSpec((SUBLANES, tm), lambda i: (0, i)), row(LANES),
                   pl.BlockSpec((1, LANES), lambda i: (0, 0))],
        out_shape=[
            jax.ShapeDtypeStruct((n, D_MODEL), F32),
            jax.ShapeDtypeStruct((n * TILE_ROWS, LANES), F32),
            jax.ShapeDtypeStruct((SUBLANES, n), I32),
            jax.ShapeDtypeStruct((n, LANES), F32),
            jax.ShapeDtypeStruct((1, LANES), F32),
        ],
        scratch_shapes=[pltpu.VMEM((1, LANES), F32)],
        compiler_params=_cparams(("arbitrary",)),
        name="mix_out_router",
    )(*parts, *ws, *x_srcs, *small)


MOE_CHUNKS = 4
MOE_TAB_RING = 4
MOE_PAD_TILES = 2


def _moe_kernel(w_tile, w_exp, w_lo, w_hi, w_first, w_last, w_newexp, w_valid,
                tab_hbm, x_hbm, wgu_ref, bgu_ref, wdn_ref, bdn_ref, yk_hbm,
                wgu_b, wdn_b, h_scr, xbuf, ybuf, tab, tsem, gsem, ssem, *, n_tiles, tm):
    j = pl.program_id(0)
    t = w_tile[j]
    b2 = t & 1
    unroll = 8

    def tab_slot(tile):
        return (tile + MOE_PAD_TILES) & (MOE_TAB_RING - 1)

    def tab_copy(tile):
        return pltpu.make_async_copy(tab_hbm.at[pl.ds((tile + MOE_PAD_TILES) * (2 * tm), 2 * tm)],
                                     tab.at[pl.ds(tab_slot(tile) * (2 * tm), 2 * tm)], tsem.at[tab_slot(tile)])

    def token(ref, tok):
        return ref.at[pl.ds(pl.multiple_of(tok * TILE_ROWS, TILE_ROWS), TILE_ROWS), :]

    def gather_copy(tile, r):
        src = tab[tab_slot(tile) * (2 * tm) + r]
        return pltpu.make_async_copy(token(x_hbm, src), token(xbuf.at[tile & 1], r), gsem.at[tile & 1])

    def scatter_copy(tile, r):
        dst = tab[tab_slot(tile) * (2 * tm) + tm + r]
        return pltpu.make_async_copy(token(ybuf.at[tile & 1], r), token(yk_hbm, dst), ssem.at[tile & 1])

    def start_all(copy, tile):
        def body(i, carry):
            for u in range(unroll):
                copy(tile, i * unroll + u).start()
            return carry

        lax.fori_loop(0, tm // unroll, body, 0)

    def wait_gather(tile):
        pltpu.make_async_copy(x_hbm.at[pl.ds(0, tm * TILE_ROWS), :], xbuf.at[tile & 1], gsem.at[tile & 1]).wait()

    def wait_scatter(tile):
        pltpu.make_async_copy(ybuf.at[tile & 1], yk_hbm.at[pl.ds(0, tm * TILE_ROWS), :], ssem.at[tile & 1]).wait()

    @pl.when(j == 0)
    def _():
        for tile in (-2, -1, 0):
            cp = tab_copy(tile)
            cp.start()
            cp.wait()
        tab_copy(1).start()
        ybuf[...] = jnp.zeros(ybuf.shape, F32)
        start_all(scatter_copy, -2)
        start_all(gather_copy, 0)

    @pl.when(w_newexp[j] == 1)
    def _():
        wgu_b[...] = wgu_ref[0, 0].astype(BF16)
        wdn_b[...] = wdn_ref[0, 0].astype(BF16)

    chunk = D_FF // MOE_CHUNKS
    rows_per_chunk = tm // MOE_CHUNKS

    def experts(first_item):
        if first_item:
            wait_gather(t)
            tab_copy(t + 1).wait()
            tab_copy(t + 2).start()
        x = _load_token_tiles(xbuf.at[b2], tm, BF16)
        bgu = bgu_ref[0, 0]
        for c in range(MOE_CHUNKS):
            cg = slice(c * chunk, (c + 1) * chunk)
            cu = slice(D_FF + c * chunk, D_FF + (c + 1) * chunk)
            g = jnp.minimum(_dot(x, wgu_b[:, cg]) + bgu[:, cg], SWIGLU_LIMIT)
            u = jnp.clip(_dot(x, wgu_b[:, cu]) + bgu[:, cu], -SWIGLU_LIMIT, SWIGLU_LIMIT)
            h_scr[:, cg] = ((u + 1.0) * g * jax.nn.sigmoid(SWIGLU_ALPHA * g)).astype(BF16)
            if first_item:
                for r in range(c * rows_per_chunk, (c + 1) * rows_per_chunk):
                    gather_copy(t + 1, r).start()
        if first_item:
            wait_scatter(t - 2)
        else:
            row = lax.broadcasted_iota(I32, (tm, 1), 0)
            mine = (row >= w_lo[j]) & (row < w_hi[j])
        h = h_scr[...]
        bdn = bdn_ref[0, 0]
        out = ybuf.at[b2]
        for c in range(MOE_CHUNKS):
            cy = slice(c * chunk, (c + 1) * chunk)
            y = _dot(h, wdn_b[:, cy]) + bdn[:, cy]
            for s in range(chunk // LANES):
                piece = y[:, s * LANES:(s + 1) * LANES]
                rows = pl.ds(c * (chunk // LANES) + s, tm, stride=TILE_ROWS)
                out[rows, :] = piece if first_item else jnp.where(mine, piece, out[rows, :])
            if first_item:
                for r in range(c * rows_per_chunk, (c + 1) * rows_per_chunk):
                    scatter_copy(t - 1, r).start(priority=r % 2)

    @pl.when(w_first[j] == 1)
    def _():
        experts(True)

    @pl.when((w_valid[j] == 1) & (w_first[j] == 0))
    def _():
        experts(False)

    @pl.when((w_last[j] == 1) & (t == n_tiles - 1))
    def _():
        start_all(scatter_copy, t)
        wait_scatter(t - 1)
        wait_scatter(t)
        wait_gather(t + 1)
        tab_copy(t + 2).wait()


def _moe_experts(l, work, tab, x1t, w_gu, b_gu, w_dn, b_dn):
    n = x1t.shape[0] // TILE_ROWS
    tm = MOE_TILE
    n_tiles = n * TOP_K // tm
    n_work = work[0].shape[0]
    by_expert = lambda j, wt, we, *_: (l, we[j], 0, 0)
    grid_spec = pltpu.PrefetchScalarGridSpec(
        num_scalar_prefetch=8,
        grid=(n_work,),
        in_specs=[
            _ANY, _ANY,
            pl.BlockSpec((1, 1, D_MODEL, 2 * D_FF), by_expert),
            pl.BlockSpec((1, 1, 1, 2 * D_FF), by_expert),
            pl.BlockSpec((1, 1, D_FF, D_MODEL), by_expert),
            pl.BlockSpec((1, 1, 1, D_MODEL), by_expert),
        ],
        out_specs=_ANY,
        scratch_shapes=[
            pltpu.VMEM((D_MODEL, 2 * D_FF), BF16), pltpu.VMEM((D_FF, D_MODEL), BF16),
            pltpu.VMEM((tm, D_FF), BF16),
            pltpu.VMEM((2, tm * TILE_ROWS, LANES), F32),
            pltpu.VMEM((2, tm * TILE_ROWS, LANES), F32),
            pltpu.SMEM((MOE_TAB_RING * 2 * tm,), I32),
            pltpu.SemaphoreType.DMA((MOE_TAB_RING,)), pltpu.SemaphoreType.DMA((2,)), pltpu.SemaphoreType.DMA((2,)),
        ],
    )
    return pl.pallas_call(
        functools.partial(_moe_kernel, n_tiles=n_tiles, tm=tm),
        grid_spec=grid_spec,
        out_shape=jax.ShapeDtypeStruct(((TOP_K * n + MOE_PAD_TILES * tm) * TILE_ROWS, LANES), F32),
        compiler_params=_cparams(("arbitrary",)),
        name="moe_experts",
    )(*work, tab, x1t, w_gu, b_gu[:, :, None, :], w_dn, b_dn[:, :, None, :])


def _moe_work_items(counts, n_slots):
    tm = MOE_TILE
    n_tiles = n_slots // tm
    n_work = n_tiles + N_EXPERTS
    ends = jnp.cumsum(counts)
    offs = ends - counts
    cand = jnp.concatenate([jnp.arange(n_tiles, dtype=I32) * tm, offs])
    ci = jnp.arange(n_work, dtype=I32)
    less = (cand[None, :] < cand[:, None]) | ((cand[None, :] == cand[:, None]) & (ci[None, :] < ci[:, None]))
    order = jnp.sum(less.astype(I32), axis=1)
    starts = jnp.sum(jnp.where(order[None, :] == ci[:, None], cand[None, :], 0), axis=1)
    stops = jnp.concatenate([starts[1:], jnp.array([n_slots], I32)])
    valid = stops > starts
    vpos = jnp.cumsum(valid.astype(I32)) - 1
    n_valid = jnp.sum(valid.astype(I32))
    want = jnp.minimum(ci, n_valid - 1)
    pick = valid[None, :] & (vpos[None, :] == want[:, None])
    starts = jnp.sum(jnp.where(pick, starts[None, :], 0), axis=1)
    stops = jnp.sum(jnp.where(pick, stops[None, :], 0), axis=1)
    is_valid = ci < n_valid
    tile = starts // tm
    expert = jnp.minimum(jnp.sum((ends[None, :] <= starts[:, None]).astype(I32), axis=1), N_EXPERTS - 1)
    prev = lambda a: jnp.concatenate([jnp.array([-1], I32), a[:-1]])
    nxt = lambda a: jnp.concatenate([a[1:], jnp.array([-1], I32)])
    first = (tile != prev(tile)) & is_valid
    last = ((tile != nxt(tile)) | (ci == n_valid - 1)) & is_valid
    newexp = (expert != prev(expert)) & is_valid
    as_i = lambda a: a.astype(I32)
    return offs, (tile, expert, starts - tile * tm, stops - tile * tm, as_i(first), as_i(last), as_i(newexp),
                  as_i(is_valid))


def _ffn_ln_kernel(x_ref, *rest, split_tile):
    yk_refs = rest[:TOP_K]
    gate_ref, g_ref, b_ref = rest[TOP_K:TOP_K + 3]
    out_refs = rest[TOP_K + 3:]
    tm = x_ref.shape[0]
    gate = gate_ref[...]
    ff = gate[:, 0:1] * _load_token_tiles(yk_refs[0], tm)
    for kk in range(1, TOP_K):
        ff = ff + gate[:, kk:kk + 1] * _load_token_tiles(yk_refs[kk], tm)
    out = _layer_norm(DN_ALPHA * x_ref[...] + ff, g_ref[...], b_ref[...])
    if split_tile is None:
        out_refs[0][...] = out
    else:
        @pl.when(pl.program_id(0) < split_tile)
        def _():
            out_refs[0][...] = out.reshape(out_refs[0].shape)

        @pl.when(pl.program_id(0) >= split_tile)
        def _():
            out_refs[1][...] = out


def _ffn_ln(x, yk, gate, ln_g, ln_b, split=None):
    n = x.shape[0]
    tm = TOKEN_TILE
    pick = lambda kk: pl.BlockSpec((tm * TILE_ROWS, LANES), lambda i: (kk * (n // tm) + i, 0))
    row = lambda c: pl.BlockSpec((tm, c), lambda i: (i, 0))
    vec = pl.BlockSpec((1, D_MODEL), lambda i: (0, 0))
    if split is None:
        split_tile = None
        out_specs = row(D_MODEL)
        out_shape = jax.ShapeDtypeStruct((n, D_MODEL), F32)
    else:
        b, t = split
        chunk = tm // b
        split_tile = t // chunk
        out_specs = [pl.BlockSpec((b, chunk, D_MODEL), lambda i: (0, jnp.minimum(i, split_tile - 1), 0)),
                     pl.BlockSpec((tm, D_MODEL), lambda i: (jnp.maximum(i - split_tile, 0), 0))]
        out_shape = [jax.ShapeDtypeStruct((b, t, D_MODEL), F32), jax.ShapeDtypeStruct((n - b * t, D_MODEL), F32)]
    return pl.pallas_call(
        functools.partial(_ffn_ln_kernel, split_tile=split_tile),
        grid=(n // tm,),
        in_specs=[row(D_MODEL)] + [pick(kk) for kk in range(TOP_K)] + [row(LANES), vec, vec],
        out_specs=out_specs,
        out_shape=out_shape,
        compiler_params=_cparams(("arbitrary",)),
        name="ffn_ln",
    )(x, *([yk] * TOP_K), gate, ln_g, ln_b)


def _moe_block(l, x1, x1t, route_t, gate, counts, ln_g, ln_b, w_gu, b_gu, w_dn, b_dn, split=None):
    n = x1.shape[0]
    n_slots = n * TOP_K
    tm = MOE_TILE
    offs, work = _moe_work_items(counts, n_slots)
    idx_t, rank_t = route_t[:TOP_K], route_t[TOP_K:]
    experts = jnp.arange(N_EXPERTS, dtype=I32)
    slot_t = rank_t + jnp.sum(jnp.where(idx_t[..., None] == experts, offs, 0), axis=-1)
    pair = jnp.argsort(slot_t.reshape(-1)).astype(I32)
    spare = n_slots + jnp.arange(MOE_PAD_TILES * tm, dtype=I32).reshape(MOE_PAD_TILES, tm)
    first = jnp.arange(MOE_PAD_TILES * tm, dtype=I32).reshape(MOE_PAD_TILES, tm) % tm
    src = jnp.concatenate([first, (pair % n).reshape(-1, tm), first], axis=0)
    dst = jnp.concatenate([spare, pair.reshape(-1, tm), spare], axis=0)
    tab = jnp.concatenate([src, dst], axis=1).reshape(-1)
    yk = _moe_experts(l, work, tab, x1t, w_gu, b_gu, w_dn, b_dn)
    return _ffn_ln(x1, yk, gate, ln_g[l][None], ln_b[l][None], split)


def _cv_in_kernel(x_ref, w_ref, b_ref, g_ref):
    xb = x_ref[...].astype(BF16)
    a = _dot(xb, w_ref[:, :CONV_W]) + b_ref[:, :CONV_W]
    gate = _dot(xb, w_ref[:, CONV_W:]) + b_ref[:, CONV_W:]
    _store_token_tiles(g_ref, a * jax.nn.sigmoid(gate))


def _cv_in_proj(x, w, b):
    n = x.shape[0]
    tm = TOKEN_TILE
    return pl.pallas_call(
        _cv_in_kernel,
        grid=(n // tm,),
        in_specs=[pl.BlockSpec((tm, D_MODEL), lambda i: (i, 0)), _full(w, 1), _full(b, 1)],
        out_specs=pl.BlockSpec((tm * TILE_ROWS, LANES), lambda i: (i, 0)),
        out_shape=jax.ShapeDtypeStruct((n * TILE_ROWS, LANES), F32),
        compiler_params=_cparams(("parallel",)),
        name="cv_in_proj",
    )(x, w, b)


CONV_GROUP = 16


def _dwconv_kernel(g_ref, past_ref, w_ref, wb_ref, lg_ref, lb_ref, *rest, bb, tt, n_prev):
    c_ref, ext, conv = rest[n_prev:]
    j = pl.program_id(1)
    halo = CONV_HALO * TILE_ROWS

    @pl.when(j == 0)
    def _():
        for b in range(bb):
            history = past_ref[b]
            for s in range(TILE_ROWS):
                rows = pl.ds((CONV_HALO - CONV_STATE) * TILE_ROWS + s, CONV_STATE, stride=TILE_ROWS)
                ext.at[b][rows, :] = history[:, s * LANES:(s + 1) * LANES]

    ext[:, halo:, :] = g_ref[...].reshape(bb, tt * TILE_ROWS, LANES)
    group = min(CONV_GROUP, tt)
    first_tap = CONV_HALO - CONV_STATE

    for b in range(bb):
        def per_group(gi, carry):
            t0 = gi * group
            acc = None
            for d in range(CONV_K):
                rows = pl.ds(pl.multiple_of((first_tap + d + t0) * TILE_ROWS, TILE_ROWS), group * TILE_ROWS)
                window = ext[b, rows, :].reshape(group, TILE_ROWS, LANES)
                term = window * w_ref[d * TILE_ROWS:(d + 1) * TILE_ROWS, :][None]
                acc = term if acc is None else acc + term
            out_rows = pl.ds(pl.multiple_of((b * tt + t0) * TILE_ROWS, TILE_ROWS), group * TILE_ROWS)
            conv[out_rows, :] = acc.reshape(group * TILE_ROWS, LANES)
            return carry

        lax.fori_loop(0, tt // group, per_group, 0)

    c = _layer_norm(_load_token_tiles(conv, bb * tt) + wb_ref[...], lg_ref[...], lb_ref[...])
    c_ref[...] = c * jax.nn.sigmoid(c)
    ext[:, 0:halo, :] = ext[:, tt * TILE_ROWS:tt * TILE_ROWS + halo, :]


def _dwconv(g, past, dw_w, dw_b, ln_g, ln_b, *, row0, b, t, bb, tt, prev):
    vec = pl.BlockSpec((1, CONV_W), lambda i, j: (0, 0))
    return _seq_call(
        functools.partial(_dwconv_kernel, bb=bb, tt=tt), name="dwconv_ln_swish",
        row0=row0, b=b, t=t, bb=bb, tt=tt, n_rows=g.shape[0] // TILE_ROWS,
        seq_ins=[g], seq_widths=[None],
        other_ins=[past, dw_w, dw_b, ln_g, ln_b],
        other_specs=[pl.BlockSpec((bb, CONV_STATE, CONV_W), lambda i, j: (i, 0, 0)), _full(dw_w, 2),
                     vec, vec, vec],
        out_widths=[CONV_W], extra_out_shapes=[], extra_out_specs=[],
        scratch=[pltpu.VMEM((bb, (CONV_HALO + tt) * TILE_ROWS, LANES), F32),
                 pltpu.VMEM((bb * tt * TILE_ROWS, LANES), F32)],
        prev_outs=prev)[0]


def _route_args(w_router, b_router, l):
    w_r = jnp.pad(w_router[l], ((0, 0), (0, LANES - N_EXPERTS))).astype(BF16)
    b_r = jnp.pad(b_router[l], (0, LANES - N_EXPERTS), constant_values=NEG_BIG)[None]
    return w_r, b_r


def kernel(x_prompt, x_sample, state_pool, state_mlstm_C, state_mlstm_n, state_mlstm_m, state_conv, w_in_pm, b_if, w_pool, pool_scale, hn_g, w_out_pm, b_out_pm, w_in_cv, b_in_cv, dw_w, dw_b, cln_g, cln_b, w_out_cv, b_out_cv, ln_mix_g, ln_mix_b, ln_ffn_g, ln_ffn_b, w_router, b_router, w_gu, b_gu, w_dn, b_dn):
    bp, tp, _ = x_prompt.shape
    bs, ts, _ = x_sample.shape
    n_p = bp * tp
    n = n_p + bs * ts
    x_srcs = [x_prompt, x_sample.reshape(bs * ts, D_MODEL)]
    chunk = TOKEN_TILE // bp
    assert chunk == MLSTM_CHUNK and tp % chunk == 0 and chunk >= max(POOL_STATE, CONV_STATE)

    groups = (
        dict(row0=0, b=bp, t=tp, pos0=0, pool_bb=bp, pool_tt=chunk, l=chunk, mlstm_bb=bp, mlstm_nc=1, conv_bb=bp,
             conv_tt=chunk, chunk_major=True,
             past_pool=jnp.zeros((bp, POOL_STATE, POOL_W), F32),
             c0=jnp.zeros((bp, MLSTM_HEADS, MLSTM_DH, MLSTM_DH), F32),
             n0=jnp.zeros((bp, MLSTM_HEADS, MLSTM_DH), F32),
             m0=jnp.zeros((bp, MLSTM_HEADS), F32),
             past_conv=jnp.zeros((bp, CONV_STATE, CONV_W), F32)),
        dict(row0=n_p, b=bs, t=ts, pos0=PAST_LEN, pool_bb=16, pool_tt=ts, l=ts, mlstm_bb=8, mlstm_nc=1, conv_bb=8,
             conv_tt=ts, chunk_major=False,
             past_pool=state_pool[0], c0=state_mlstm_C[0], n0=state_mlstm_n[0], m0=state_mlstm_m[0],
             past_conv=state_conv[0]),
    )

    def tail_state(past, a, gr, keep, tiled=False):
        b, t, row0 = gr["b"], gr["t"], gr["row0"]
        rpt = TILE_ROWS if tiled else 1
        width = a.shape[-1] * rpt
        if gr["chunk_major"]:
            ends = [row0 + (t - chunk) * b + (i + 1) * chunk for i in range(b)]
            parts = [a[(e - keep) * rpt:e * rpt] for e in ends]
            return jnp.stack(parts).reshape(b, keep, width)
        new = a[row0 * rpt:(row0 + b * t) * rpt].reshape(b, t, width)
        return jnp.concatenate([past[:, t:], new], axis=1)

    w_in = w_in_pm[0]
    w_main = w_in[:, :POOL_W + 4 * MLSTM_W].astype(BF16)
    n_gate = 2 * MLSTM_HEADS
    w_gate = jnp.pad(w_in[:, POOL_W + 4 * MLSTM_W:], ((0, 0), (0, LANES - n_gate))).astype(BF16)
    b_gate = jnp.pad(b_if[0], (0, LANES - n_gate))[None]
    w_out = w_out_pm[0].astype(BF16)
    w_r, b_r = _route_args(w_router, b_router, 0)

    u, q, k, v, o, g = _pm_in_proj(x_srcs, w_main, w_gate, b_gate)
    y_m, states = o, []
    for gr in groups:
        b, t = gr["b"], gr["t"]
        n0 = gr["n0"][:, :, None, :]
        m0 = jnp.broadcast_to(gr["m0"][:, :, None, None], (b, MLSTM_HEADS, 1, LANES))
        y_m, c_new, n_new, m_new = _mlstm(q, k, v, y_m, g, gr["c0"], n0, m0, hn_g[0][None], row0=gr["row0"], b=b,
                                          t=t, bb=gr["mlstm_bb"], l=gr["l"], nc=gr["mlstm_nc"])
        states.append((tail_state(gr["past_pool"], u, gr, POOL_STATE), c_new, n_new[:, :, 0, :], m_new[:, :, 0, 0]))
    y_pool = q
    for gr in groups:
        past = jnp.pad(gr["past_pool"], ((0, 0), (POOL_HALO - POOL_STATE, 0), (0, 0)))
        y_pool = _pool_mix(u, past, w_pool[0].astype(BF16), pool_scale[0][None], row0=gr["row0"], b=gr["b"],
                           t=gr["t"], bb=gr["pool_bb"], tt=gr["pool_tt"], pos0=gr["pos0"], prev=[y_pool])
    x1, x1t, route_t, gate, counts = _mix_out([y_pool, y_m], [w_out[:POOL_W], w_out[POOL_W:]], x_srcs,
                                              b_out_pm[0][None], ln_mix_g[0][None], ln_mix_b[0][None], w_r, b_r)
    x2 = _moe_block(0, x1, x1t, route_t, gate, counts[0, :N_EXPERTS].astype(I32), ln_ffn_g, ln_ffn_b,
                    w_gu, b_gu, w_dn, b_dn)

    w_r, b_r = _route_args(w_router, b_router, 1)
    dw_w_p = jnp.pad(dw_w[0], ((0, CONV_HALO - CONV_K), (0, 0))).reshape(CONV_HALO * TILE_ROWS, LANES)
    gl = _cv_in_proj(x2, w_in_cv[0].astype(BF16), b_in_cv[0][None])
    c, conv_states = x1, []
    for gr in groups:
        c = _dwconv(gl, gr["past_conv"], dw_w_p, dw_b[0][None], cln_g[0][None], cln_b[0][None], row0=gr["row0"], b=gr["b"],
                    t=gr["t"], bb=gr["conv_bb"], tt=gr["conv_tt"], prev=[c])
        conv_states.append(tail_state(gr["past_conv"], gl, gr, CONV_STATE, tiled=True))
    x3, x3t, route_t, gate, counts = _mix_out([c], [w_out_cv[0].astype(BF16)], [x2], b_out_cv[0][None],
                                              ln_mix_g[1][None], ln_mix_b[1][None], w_r, b_r)
    y_p, y_s = _moe_block(1, x3, x3t, route_t, gate, counts[0, :N_EXPERTS].astype(I32), ln_ffn_g, ln_ffn_b,
                          w_gu, b_gu, w_dn, b_dn, split=(bp, tp))

    (pool_p, c_p, n_p_, m_p), (pool_s, c_s, n_s, m_s) = states
    return (y_p, y_s.reshape(bs, ts, D_MODEL),
            pool_p[None], pool_s[None], c_p[None], c_s[None], n_p_[None], n_s[None], m_p[None], m_s[None],
            conv_states[0][None], conv_states[1][None])
```

```python
import functools

import jax
import jax.numpy as jnp
from jax import lax
from jax.experimental import pallas as pl
from jax.experimental.pallas import tpu as pltpu

F32 = jnp.float32
BF16 = jnp.bfloat16
I32 = jnp.int32

D_MODEL = 1024
DEPTH = 2
POOL_W = 512
POOL_WINDOWS = (2, 4, 8, 16)
POOL_GW = 128
POOL_STATE = 15
POOL_HALO = 16
MLSTM_HEADS = 4
MLSTM_W = 512
MLSTM_DH = 128
MLSTM_CHUNK = 64
MLSTM_LOCKSTEP_SEQS = 2
CONV_W = 1024
CONV_K = 31
CONV_STATE = 30
CONV_HALO = 32
N_EXPERTS = 32
TOP_K = 4
D_FF = 1024
SWIGLU_LIMIT = 7.0
SWIGLU_ALPHA = 1.702
LN_EPS = 1e-5
DN_ALPHA = (2 * DEPTH) ** 0.25
PAST_LEN = 16384

LANES = 128
SUBLANES = 8
TOKEN_TILE = 512
MOE_TILE = 512
NEG_BIG = -1e30
VMEM_LIMIT = 56 * 1024 * 1024


def _cparams(sem, vmem=VMEM_LIMIT):
    return pltpu.CompilerParams(dimension_semantics=sem, vmem_limit_bytes=vmem)


def _layer_norm(x, g, b):
    mu = jnp.mean(x, -1, keepdims=True)
    xc = x - mu
    var = jnp.mean(xc * xc, -1, keepdims=True)
    return xc * lax.rsqrt(var + LN_EPS) * g + b


def _log_sigmoid(x):
    return jnp.minimum(x, 0.0) - jnp.log1p(jnp.exp(-jnp.abs(x)))


def _dot(a, b):
    return jnp.dot(a, b, preferred_element_type=F32)


def _bf16_pieces(x):
    hi = x.astype(BF16)
    rest = x - hi.astype(F32)
    mid = rest.astype(BF16)
    return hi, mid, (rest - mid.astype(F32)).astype(BF16)


def _dot_exact_lhs(a, x, dims=(((1,), (0,)), ((), ()))):
    a = a.astype(BF16)
    return sum(lax.dot_general(a, p, dims, preferred_element_type=F32) for p in _bf16_pieces(x))


def _rows_of(cols):
    sel = (lax.broadcasted_iota(I32, (SUBLANES, LANES), 0) == lax.broadcasted_iota(I32, (SUBLANES, LANES), 1))
    return _dot_exact_lhs(sel, cols, (((1,), (1,)), ((), ())))


TILE_ROWS = D_MODEL // LANES


def _store_token_tiles(ref, val):
    tokens = val.shape[0]
    for s in range(TILE_ROWS):
        ref[pl.ds(s, tokens, stride=TILE_ROWS), :] = val[:, s * LANES:(s + 1) * LANES]


def _load_token_tiles(ref, tokens, dtype=F32):
    return jnp.concatenate([ref[pl.ds(s, tokens, stride=TILE_ROWS), :].astype(dtype) for s in range(TILE_ROWS)],
                           axis=1)


_ANY = pl.BlockSpec(memory_space=pl.ANY)


def _full(a, grid_rank):
    return pl.BlockSpec(a.shape, lambda *_: (0,) * a.ndim)


def _token_specs(x_srcs):
    tm = TOKEN_TILE
    if len(x_srcs) == 1:
        return [pl.BlockSpec((tm, D_MODEL), lambda i: (i, 0))], None, x_srcs[0].shape[0]
    x_p, x_s = x_srcs
    b, t, _ = x_p.shape
    chunk = tm // b
    tiles_p = t // chunk
    assert b * chunk == tm and tiles_p * chunk == t and x_s.shape[0] % tm == 0
    specs = [pl.BlockSpec((b, chunk, D_MODEL), lambda i: (0, jnp.minimum(i, tiles_p - 1), 0)),
             pl.BlockSpec((tm, D_MODEL), lambda i: (jnp.maximum(i - tiles_p, 0), 0))]
    return specs, tiles_p, b * t + x_s.shape[0]


def _token_rows(x_refs, tiles_p):
    if len(x_refs) == 1:
        return x_refs[0][...]
    x_p, x_s = x_refs
    return jnp.where(pl.program_id(0) < tiles_p, x_p[...].reshape(TOKEN_TILE, D_MODEL), x_s[...])


def _pm_in_kernel(*refs, n_x, tiles_p):
    x_refs = refs[:n_x]
    w_ref, wg_ref, bg_ref, u_ref, q_ref, k_ref, v_ref, o_ref, g_ref = refs[n_x:]
    xb = _token_rows(x_refs, tiles_p).astype(BF16)
    for j, out in enumerate((u_ref, q_ref, k_ref, v_ref, o_ref)):
        out[...] = _dot(xb, w_ref[:, j * 512:(j + 1) * 512])
    g_ref[...] = _dot(xb, wg_ref[...]) + bg_ref[...]


def _pm_in_proj(x_srcs, w_main, w_gate, b_gate):
    tm = TOKEN_TILE
    x_specs, tiles_p, n = _token_specs(x_srcs)
    row = lambda c: pl.BlockSpec((tm, c), lambda i: (i, 0))
    return pl.pallas_call(
        functools.partial(_pm_in_kernel, n_x=len(x_srcs), tiles_p=tiles_p),
        grid=(n // tm,),
        in_specs=x_specs + [_full(w_main, 1), _full(w_gate, 1), _full(b_gate, 1)],
        out_specs=[row(512)] * 5 + [row(LANES)],
        out_shape=[jax.ShapeDtypeStruct((n, 512), F32)] * 5 + [jax.ShapeDtypeStruct((n, LANES), F32)],
        compiler_params=_cparams(("arbitrary",)),
        name="pm_in_proj",
    )(*x_srcs, w_main, w_gate, b_gate)


def _seq_call(kernel, *, name, row0, b, t, bb, tt, n_rows, seq_ins, seq_widths, other_ins, other_specs,
              out_widths, extra_out_shapes, extra_out_specs, scratch, prev_outs, in_place=None):
    steps = t // tt
    blk0 = row0 // (bb * tt)

    def seq_spec(c):
        shape = (bb * tt * TILE_ROWS, LANES) if c is None else (bb * tt, c)
        return pl.BlockSpec(shape, lambda i, j: (blk0 + i * steps + j, 0))

    prev = list(prev_outs)
    first_prev = len(seq_ins) + len(other_ins)
    aliases = {first_prev + o: o for o in range(len(prev))}
    if in_place is not None:
        aliases[in_place] = 0
    return pl.pallas_call(
        functools.partial(kernel, n_prev=len(prev)),
        grid=(b // bb, steps),
        in_specs=[seq_spec(c) for c in seq_widths] + list(other_specs) + [_ANY] * len(prev),
        out_specs=[seq_spec(c) for c in out_widths] + list(extra_out_specs),
        out_shape=[jax.ShapeDtypeStruct((n_rows, c), F32) for c in out_widths] + list(extra_out_shapes),
        input_output_aliases=aliases,
        scratch_shapes=scratch,
        compiler_params=_cparams(("parallel", "arbitrary")),
        name=name,
    )(*seq_ins, *other_ins, *prev)


def _pool_kernel(u_ref, past_ref, w_ref, sc_ref, *rest, bb, tt, pos0, n_prev):
    y_ref, ext = rest[n_prev:]
    j = pl.program_id(1)

    @pl.when(j == 0)
    def _():
        ext[:, 0:POOL_HALO, :] = past_ref[...]

    u = u_ref[...].reshape(bb, tt, POOL_W)
    ext[:, POOL_HALO:POOL_HALO + tt, :] = u
    pos = pos0 + j * tt + lax.broadcasted_iota(I32, (tt, 1), 0)
    for g, w in enumerate(POOL_WINDOWS):
        lanes = slice(g * POOL_GW, (g + 1) * POOL_GW)
        wsum = ext[:, POOL_HALO:POOL_HALO + tt, lanes]
        for d in range(1, w):
            wsum = wsum + ext[:, POOL_HALO - d:POOL_HALO - d + tt, lanes]
        cnt = jnp.minimum(w, pos + 1).astype(F32)
        pooled = wsum / cnt[None] - u[:, :, lanes]
        y = _dot(pooled.reshape(bb * tt, POOL_GW).astype(BF16), w_ref[g])
        y_ref[:, lanes] = y * sc_ref[:, lanes]
    ext[:, 0:POOL_HALO, :] = ext[:, tt:tt + POOL_HALO, :]


def _pool_mix(u, past, w_pool, scale, *, row0, b, t, bb, tt, pos0, prev):
    return _seq_call(
        functools.partial(_pool_kernel, bb=bb, tt=tt, pos0=pos0), name="pool_mix",
        row0=row0, b=b, t=t, bb=bb, tt=tt, n_rows=u.shape[0],
        seq_ins=[u], seq_widths=[POOL_W],
        other_ins=[past, w_pool, scale],
        other_specs=[pl.BlockSpec((bb, POOL_HALO, POOL_W), lambda i, j: (i, 0, 0)), _full(w_pool, 2), _full(scale, 2)],
        out_widths=[POOL_W], extra_out_shapes=[], extra_out_specs=[],
        scratch=[pltpu.VMEM((bb, POOL_HALO + tt, POOL_W), F32)], prev_outs=prev)[0]


def _mlstm_kernel(q_ref, k_ref, v_ref, o_ref, g_ref, c0_ref, n0_ref, m0_ref, hng_ref, *rest, bb, l, nc, n_prev):
    y_ref, c_out, n_out, m_out, cs, ns, ms = rest[n_prev:]
    c = pl.program_id(1)

    @pl.when(c == 0)
    def _():
        cs[...] = c0_ref[...]
        ns[...] = n0_ref[...]
        ms[...] = m0_ref[...]

    row = lax.broadcasted_iota(I32, (l, l), 0)
    col = lax.broadcasted_iota(I32, (l, l), 1)
    causal = row >= col
    scale = MLSTM_DH ** -0.5
    padr = LANES - l if l < 16 else 0

    assert nc == 1
    tril = causal.astype(BF16)
    lane = lax.broadcasted_iota(I32, (l, LANES), 1)
    gate_cols, gate_rows = {}, {}

    def lockstep(chains):
        live = list(chains)
        while live:
            still = []
            for ch in live:
                try:
                    next(ch)
                    still.append(ch)
                except StopIteration:
                    pass
            live = still

    def gate_chain(b):
        gates = g_ref[b * l:(b + 1) * l, :]
        bcum = _dot_exact_lhs(tril, _log_sigmoid(gates))
        yield
        cols = jnp.where(lane < MLSTM_HEADS, gates, bcum)
        gate_cols[b] = cols
        gate_rows[b] = _rows_of(cols)
        yield

    def head_chain(b, h):
        rows_b = pl.ds(b * l, l)
        hl = slice(h * MLSTM_DH, (h + 1) * MLSTM_DH)
        cols, rows = gate_cols[b], gate_rows[b]
        i_col = cols[:, h:h + 1]
        b_col = cols[:, MLSTM_HEADS + h:MLSTM_HEADS + h + 1]
        i_row = rows[h:h + 1, :]
        b_row = rows[MLSTM_HEADS + h:MLSTM_HEADS + h + 1, :]
        c_prev, n_prev_, m_prev = cs[b, h], ns[b, h], ms[b, h][:, 0:1]
        qh = q_ref[rows_b, hl] * scale
        kh = k_ref[rows_b, hl]
        vh = v_ref[rows_b, hl]

        dmat = jnp.where(causal, b_col - b_row + i_row, -jnp.inf)
        inter = b_col + m_prev
        qk = lax.dot_general(qh, kh, (((1,), (1,)), ((), ())), preferred_element_type=F32)
        q_c = _dot(qh, c_prev)
        m_t = jnp.maximum(inter, jnp.max(dmat, axis=-1, keepdims=True))
        yield
        s = qk * jnp.exp(dmat - m_t)
        w_inter = jnp.exp(inter - m_t)
        num = _dot(s, vh) + w_inter * q_c
        den = jnp.sum(s, -1, keepdims=True) + w_inter * jnp.sum(qh * n_prev_, -1, keepdims=True)
        yield
        bound = jnp.maximum(jnp.abs(den), jnp.exp(-m_t))
        hh = num / bound
        mu = jnp.mean(hh, -1, keepdims=True)
        yield
        hc = hh - mu
        var = jnp.mean(hc * hc, -1, keepdims=True)
        yield
        hn = hc * lax.rsqrt(var + LN_EPS) * hng_ref[:, hl]
        y_ref[rows_b, hl] = jax.nn.sigmoid(o_ref[rows_b, hl]) * hn

        b_end = b_col[l - 1:l, :]
        gcol = b_end - b_col + i_col
        m_new = jnp.maximum(b_end + m_prev, jnp.max(gcol, axis=0, keepdims=True))
        kw = kh * jnp.exp(gcol - m_new)
        decay = jnp.exp(b_end + m_prev - m_new)
        if padr:
            zpad = jnp.zeros((padr, MLSTM_DH), F32)
            kv = lax.dot_general(jnp.concatenate([kw, zpad], 0), jnp.concatenate([vh, zpad], 0),
                                 (((0,), (0,)), ((), ())), preferred_element_type=F32)
        else:
            kv = lax.dot_general(kw, vh, (((0,), (0,)), ((), ())), preferred_element_type=F32)
        yield
        cs[b, h] = decay * c_prev + kv
        ns[b, h] = decay * n_prev_ + jnp.sum(kw, axis=0, keepdims=True)
        ms[b, h] = jnp.broadcast_to(m_new, (1, LANES))

    lockstep(gate_chain(b) for b in range(bb))
    for b0 in range(0, bb, MLSTM_LOCKSTEP_SEQS):
        lockstep(head_chain(b, h) for b in range(b0, min(bb, b0 + MLSTM_LOCKSTEP_SEQS)) for h in range(MLSTM_HEADS))

    @pl.when(c == pl.num_programs(1) - 1)
    def _():
        c_out[...] = cs[...]
        n_out[...] = ns[...]
        m_out[...] = ms[...]


def _mlstm(q, k, v, o, g, c0, n0, m0, hn_g, *, row0, b, t, bb, l, nc):
    st_c = pl.BlockSpec((bb, MLSTM_HEADS, MLSTM_DH, MLSTM_DH), lambda i, j: (i, 0, 0, 0))
    st_v = pl.BlockSpec((bb, MLSTM_HEADS, 1, LANES), lambda i, j: (i, 0, 0, 0))
    return _seq_call(
        functools.partial(_mlstm_kernel, bb=bb, l=l, nc=nc), name="mlstm",
        row0=row0, b=b, t=t, bb=bb, tt=nc * l, n_rows=q.shape[0],
        seq_ins=[q, k, v, o, g], seq_widths=[MLSTM_W] * 4 + [LANES],
        other_ins=[c0, n0, m0, hn_g], other_specs=[st_c, st_v, st_v, _full(hn_g, 2)],
        out_widths=[MLSTM_W],
        extra_out_shapes=[jax.ShapeDtypeStruct((b, MLSTM_HEADS, MLSTM_DH, MLSTM_DH), F32),
                          jax.ShapeDtypeStruct((b, MLSTM_HEADS, 1, LANES), F32),
                          jax.ShapeDtypeStruct((b, MLSTM_HEADS, 1, LANES), F32)],
        extra_out_specs=[st_c, st_v, st_v],
        scratch=[pltpu.VMEM((bb, MLSTM_HEADS, MLSTM_DH, MLSTM_DH), F32),
                 pltpu.VMEM((bb, MLSTM_HEADS, 1, LANES), F32),
                 pltpu.VMEM((bb, MLSTM_HEADS, 1, LANES), F32)],
        prev_outs=[], in_place=3)


def _mix_out_kernel(*refs, n_parts, n_x, tiles_p):
    parts = refs[:n_parts]
    ws = refs[n_parts:2 * n_parts]
    x_refs = refs[2 * n_parts:2 * n_parts + n_x]
    (bo_ref, g_ref, b_ref, wr_ref, br_ref,
     x1_ref, x1t_ref, route_ref, gate_ref, cnt_ref, carry) = refs[2 * n_parts + n_x:]
    i = pl.program_id(0)

    @pl.when(i == 0)
    def _():
        carry[...] = jnp.zeros_like(carry)

    mix = bo_ref[...]
    for a_ref, w_ref in zip(parts, ws):
        mix = mix + _dot(a_ref[...].astype(BF16), w_ref[...])
    x1 = _layer_norm(DN_ALPHA * _token_rows(x_refs, tiles_p) + mix, g_ref[...], b_ref[...])
    x1_ref[...] = x1
    tm = x1.shape[0]
    _store_token_tiles(x1t_ref, x1)

    logits = _dot(x1.astype(BF16), wr_ref[...]) + br_ref[...]
    lane = lax.broadcasted_iota(I32, (tm, LANES), 1)
    lane_f = lane.astype(F32)
    work = logits
    picks, vals = [], []
    onehot = jnp.zeros((tm, LANES), F32)
    for _ in range(TOP_K):
        mx = jnp.max(work, axis=-1, keepdims=True)
        pick = jnp.min(jnp.where(work == mx, lane_f, float(LANES)), axis=-1, keepdims=True)
        hit = lane_f == pick
        onehot = jnp.where(hit, 1.0, onehot)
        work = jnp.where(hit, -jnp.inf, work)
        picks.append(pick)
        vals.append(mx)
    exps = [jnp.exp(v - vals[0]) for v in vals]
    tot = exps[0] + exps[1] + exps[2] + exps[3]

    rr = lax.broadcasted_iota(I32, (tm, tm), 0)
    cc = lax.broadcasted_iota(I32, (tm, tm), 1)
    before = _dot((rr > cc).astype(BF16), onehot.astype(BF16)) + carry[...]
    route = jnp.zeros((tm, LANES), F32)
    gate_o = jnp.zeros((tm, LANES), F32)
    for kk in range(TOP_K):
        rk = jnp.sum(jnp.where(lane_f == picks[kk], before, 0.0), axis=-1, keepdims=True)
        route = jnp.where(lane == kk, picks[kk], route)
        route = jnp.where(lane == TOP_K + kk, rk, route)
        gate_o = jnp.where(lane == kk, exps[kk] / tot, gate_o)
    route_ref[...] = _rows_of(route).astype(I32)
    gate_ref[...] = gate_o
    carry[...] = carry[...] + jnp.sum(onehot, axis=0, keepdims=True)
    cnt_ref[...] = carry[...]


def _mix_out(parts, ws, x_srcs, b_out, ln_g, ln_b, w_r, b_r):
    tm = TOKEN_TILE
    x_specs, tiles_p, n = _token_specs(x_srcs)
    row = lambda c: pl.BlockSpec((tm, c), lambda i: (i, 0))
    small = [b_out, ln_g, ln_b, w_r, b_r]
    return pl.pallas_call(
        functools.partial(_mix_out_kernel, n_parts=len(parts), n_x=len(x_srcs), tiles_p=tiles_p),
        grid=(n // tm,),
        in_specs=[row(p.shape[1]) for p in parts] + [_full(w, 1) for w in ws] + x_specs
        + [_full(a, 1) for a in small],
        out_specs=[row(D_MODEL), pl.BlockSpec((tm * TILE_ROWS, LANES), lambda i: (i, 0)),
                   pl.BlockSpec((SUBLANES, tm), lambda i: (0, i)), row(LANES),
                   pl.BlockSpec((1, LANES), lambda i: (0, 0))],
        out_shape=[
            jax.ShapeDtypeStruct((n, D_MODEL), F32),
            jax.ShapeDtypeStruct((n * TILE_ROWS, LANES), F32),
            jax.ShapeDtypeStruct((SUBLANES, n), I32),
            jax.ShapeDtypeStruct((n, LANES), F32),
            jax.ShapeDtypeStruct((1, LANES), F32),
        ],
        scratch_shapes=[pltpu.VMEM((1, LANES), F32)],
        compiler_params=_cparams(("arbitrary",)),
        name="mix_out_router",
    )(*parts, *ws, *x_srcs, *small)


MOE_CHUNKS = 4
MOE_TAB_RING = 4
MOE_PAD_TILES = 2


def _moe_kernel(w_tile, w_exp, w_lo, w_hi, w_first, w_last, w_newexp, w_valid,
                tab_hbm, x_hbm, wgu_ref, bgu_ref, wdn_ref, bdn_ref, yk_hbm,
                wgu_b, wdn_b, h_scr, xbuf, ybuf, tab, tsem, gsem, ssem, *, n_tiles, tm):
    j = pl.program_id(0)
    t = w_tile[j]
    b2 = t & 1
    unroll = 8

    def tab_slot(tile):
        return (tile + MOE_PAD_TILES) & (MOE_TAB_RING - 1)

    def tab_copy(tile):
        return pltpu.make_async_copy(tab_hbm.at[pl.ds((tile + MOE_PAD_TILES) * (2 * tm), 2 * tm)],
                                     tab.at[pl.ds(tab_slot(tile) * (2 * tm), 2 * tm)], tsem.at[tab_slot(tile)])

    def token(ref, tok):
        return ref.at[pl.ds(pl.multiple_of(tok * TILE_ROWS, TILE_ROWS), TILE_ROWS), :]

    def gather_copy(tile, r):
        src = tab[tab_slot(tile) * (2 * tm) + r]
        return pltpu.make_async_copy(token(x_hbm, src), token(xbuf.at[tile & 1], r), gsem.at[tile & 1])

    def scatter_copy(tile, r):
        dst = tab[tab_slot(tile) * (2 * tm) + tm + r]
        return pltpu.make_async_copy(token(ybuf.at[tile & 1], r), token(yk_hbm, dst), ssem.at[tile & 1])

    def start_all(copy, tile):
        def body(i, carry):
            for u in range(unroll):
                copy(tile, i * unroll + u).start()
            return carry

        lax.fori_loop(0, tm // unroll, body, 0)

    def wait_gather(tile):
        pltpu.make_async_copy(x_hbm.at[pl.ds(0, tm * TILE_ROWS), :], xbuf.at[tile & 1], gsem.at[tile & 1]).wait()

    def wait_scatter(tile):
        pltpu.make_async_copy(ybuf.at[tile & 1], yk_hbm.at[pl.ds(0, tm * TILE_ROWS), :], ssem.at[tile & 1]).wait()

    @pl.when(j == 0)
    def _():
        for tile in (-2, -1, 0):
            cp = tab_copy(tile)
            cp.start()
            cp.wait()
        tab_copy(1).start()
        ybuf[...] = jnp.zeros(ybuf.shape, F32)
        start_all(scatter_copy, -2)
        start_all(gather_copy, 0)

    @pl.when(w_newexp[j] == 1)
    def _():
        wgu_b[...] = wgu_ref[0, 0].astype(BF16)
        wdn_b[...] = wdn_ref[0, 0].astype(BF16)

    chunk = D_FF // MOE_CHUNKS
    rows_per_chunk = tm // MOE_CHUNKS

    def experts(first_item):
        if first_item:
            wait_gather(t)
            tab_copy(t + 1).wait()
            tab_copy(t + 2).start()
        x = _load_token_tiles(xbuf.at[b2], tm, BF16)
        bgu = bgu_ref[0, 0]
        for c in range(MOE_CHUNKS):
            cg = slice(c * chunk, (c + 1) * chunk)
            cu = slice(D_FF + c * chunk, D_FF + (c + 1) * chunk)
            g = jnp.minimum(_dot(x, wgu_b[:, cg]) + bgu[:, cg], SWIGLU_LIMIT)
            u = jnp.clip(_dot(x, wgu_b[:, cu]) + bgu[:, cu], -SWIGLU_LIMIT, SWIGLU_LIMIT)
            h_scr[:, cg] = ((u + 1.0) * g * jax.nn.sigmoid(SWIGLU_ALPHA * g)).astype(BF16)
            if first_item:
                for r in range(c * rows_per_chunk, (c + 1) * rows_per_chunk):
                    gather_copy(t + 1, r).start(priority=r % 2)
        if first_item:
            wait_scatter(t - 2)
        else:
            row = lax.broadcasted_iota(I32, (tm, 1), 0)
            mine = (row >= w_lo[j]) & (row < w_hi[j])
        h = h_scr[...]
        bdn = bdn_ref[0, 0]
        out = ybuf.at[b2]
        for c in range(MOE_CHUNKS):
            cy = slice(c * chunk, (c + 1) * chunk)
            y = _dot(h, wdn_b[:, cy]) + bdn[:, cy]
            for s in range(chunk // LANES):
                piece = y[:, s * LANES:(s + 1) * LANES]
                rows = pl.ds(c * (chunk // LANES) + s, tm, stride=TILE_ROWS)
                out[rows, :] = piece if first_item else jnp.where(mine, piece, out[rows, :])
            if first_item:
                for r in range(c * rows_per_chunk, (c + 1) * rows_per_chunk):
                    scatter_copy(t - 1, r).start(priority=r % 2)

    @pl.when(w_first[j] == 1)
    def _():
        experts(True)

    @pl.when((w_valid[j] == 1) & (w_first[j] == 0))
    def _():
        experts(False)

    @pl.when((w_last[j] == 1) & (t == n_tiles - 1))
    def _():
        start_all(scatter_copy, t)
        wait_scatter(t - 1)
        wait_scatter(t)
        wait_gather(t + 1)
        tab_copy(t + 2).wait()


def _moe_experts(l, work, tab, x1t, w_gu, b_gu, w_dn, b_dn):
    n = x1t.shape[0] // TILE_ROWS
    tm = MOE_TILE
    n_tiles = n * TOP_K // tm
    n_work = work[0].shape[0]
    by_expert = lambda j, wt, we, *_: (l, we[j], 0, 0)
    grid_spec = pltpu.PrefetchScalarGridSpec(
        num_scalar_prefetch=8,
        grid=(n_work,),
        in_specs=[
            _ANY, _ANY,
            pl.BlockSpec((1, 1, D_MODEL, 2 * D_FF), by_expert),
            pl.BlockSpec((1, 1, 1, 2 * D_FF), by_expert),
            pl.BlockSpec((1, 1, D_FF, D_MODEL), by_expert),
            pl.BlockSpec((1, 1, 1, D_MODEL), by_expert),
        ],
        out_specs=_ANY,
        scratch_shapes=[
            pltpu.VMEM((D_MODEL, 2 * D_FF), BF16), pltpu.VMEM((D_FF, D_MODEL), BF16),
            pltpu.VMEM((tm, D_FF), BF16),
            pltpu.VMEM((2, tm * TILE_ROWS, LANES), F32),
            pltpu.VMEM((2, tm * TILE_ROWS, LANES), F32),
            pltpu.SMEM((MOE_TAB_RING * 2 * tm,), I32),
            pltpu.SemaphoreType.DMA((MOE_TAB_RING,)), pltpu.SemaphoreType.DMA((2,)), pltpu.SemaphoreType.DMA((2,)),
        ],
    )
    return pl.pallas_call(
        functools.partial(_moe_kernel, n_tiles=n_tiles, tm=tm),
        grid_spec=grid_spec,
        out_shape=jax.ShapeDtypeStruct(((TOP_K * n + MOE_PAD_TILES * tm) * TILE_ROWS, LANES), F32),
        compiler_params=_cparams(("arbitrary",)),
        name="moe_experts",
    )(*work, tab, x1t, w_gu, b_gu[:, :, None, :], w_dn, b_dn[:, :, None, :])


def _moe_work_items(counts, n_slots):
    tm = MOE_TILE
    n_tiles = n_slots // tm
    n_work = n_tiles + N_EXPERTS
    ends = jnp.cumsum(counts)
    offs = ends - counts
    cand = jnp.concatenate([jnp.arange(n_tiles, dtype=I32) * tm, offs])
    ci = jnp.arange(n_work, dtype=I32)
    less = (cand[None, :] < cand[:, None]) | ((cand[None, :] == cand[:, None]) & (ci[None, :] < ci[:, None]))
    order = jnp.sum(less.astype(I32), axis=1)
    starts = jnp.sum(jnp.where(order[None, :] == ci[:, None], cand[None, :], 0), axis=1)
    stops = jnp.concatenate([starts[1:], jnp.array([n_slots], I32)])
    valid = stops > starts
    vpos = jnp.cumsum(valid.astype(I32)) - 1
    n_valid = jnp.sum(valid.astype(I32))
    want = jnp.minimum(ci, n_valid - 1)
    pick = valid[None, :] & (vpos[None, :] == want[:, None])
    starts = jnp.sum(jnp.where(pick, starts[None, :], 0), axis=1)
    stops = jnp.sum(jnp.where(pick, stops[None, :], 0), axis=1)
    is_valid = ci < n_valid
    tile = starts // tm
    expert = jnp.minimum(jnp.sum((ends[None, :] <= starts[:, None]).astype(I32), axis=1), N_EXPERTS - 1)
    prev = lambda a: jnp.concatenate([jnp.array([-1], I32), a[:-1]])
    nxt = lambda a: jnp.concatenate([a[1:], jnp.array([-1], I32)])
    first = (tile != prev(tile)) & is_valid
    last = ((tile != nxt(tile)) | (ci == n_valid - 1)) & is_valid
    newexp = (expert != prev(expert)) & is_valid
    as_i = lambda a: a.astype(I32)
    return offs, (tile, expert, starts - tile * tm, stops - tile * tm, as_i(first), as_i(last), as_i(newexp),
                  as_i(is_valid))


def _ffn_ln_kernel(x_ref, *rest, split_tile):
    yk_refs = rest[:TOP_K]
    gate_ref, g_ref, b_ref = rest[TOP_K:TOP_K + 3]
    out_refs = rest[TOP_K + 3:]
    tm = x_ref.shape[0]
    gate = gate_ref[...]
    ff = gate[:, 0:1] * _load_token_tiles(yk_refs[0], tm)
    for kk in range(1, TOP_K):
        ff = ff + gate[:, kk:kk + 1] * _load_token_tiles(yk_refs[kk], tm)
    out = _layer_norm(DN_ALPHA * x_ref[...] + ff, g_ref[...], b_ref[...])
    if split_tile is None:
        out_refs[0][...] = out
    else:
        @pl.when(pl.program_id(0) < split_tile)
        def _():
            out_refs[0][...] = out.reshape(out_refs[0].shape)

        @pl.when(pl.program_id(0) >= split_tile)
        def _():
            out_refs[1][...] = out


def _ffn_ln(x, yk, gate, ln_g, ln_b, split=None):
    n = x.shape[0]
    tm = TOKEN_TILE
    pick = lambda kk: pl.BlockSpec((tm * TILE_ROWS, LANES), lambda i: (kk * (n // tm) + i, 0))
    row = lambda c: pl.BlockSpec((tm, c), lambda i: (i, 0))
    vec = pl.BlockSpec((1, D_MODEL), lambda i: (0, 0))
    if split is None:
        split_tile = None
        out_specs = row(D_MODEL)
        out_shape = jax.ShapeDtypeStruct((n, D_MODEL), F32)
    else:
        b, t = split
        chunk = tm // b
        split_tile = t // chunk
        out_specs = [pl.BlockSpec((b, chunk, D_MODEL), lambda i: (0, jnp.minimum(i, split_tile - 1), 0)),
                     pl.BlockSpec((tm, D_MODEL), lambda i: (jnp.maximum(i - split_tile, 0), 0))]
        out_shape = [jax.ShapeDtypeStruct((b, t, D_MODEL), F32), jax.ShapeDtypeStruct((n - b * t, D_MODEL), F32)]
    return pl.pallas_call(
        functools.partial(_ffn_ln_kernel, split_tile=split_tile),
        grid=(n // tm,),
        in_specs=[row(D_MODEL)] + [pick(kk) for kk in range(TOP_K)] + [row(LANES), vec, vec],
        out_specs=out_specs,
        out_shape=out_shape,
        compiler_params=_cparams(("arbitrary",)),
        name="ffn_ln",
    )(x, *([yk] * TOP_K), gate, ln_g, ln_b)


def _moe_block(l, x1, x1t, route_t, gate, counts, ln_g, ln_b, w_gu, b_gu, w_dn, b_dn, split=None):
    n = x1.shape[0]
    n_slots = n * TOP_K
    tm = MOE_TILE
    offs, work = _moe_work_items(counts, n_slots)
    idx_t, rank_t = route_t[:TOP_K], route_t[TOP_K:]
    experts = jnp.arange(N_EXPERTS, dtype=I32)
    slot_t = rank_t + jnp.sum(jnp.where(idx_t[..., None] == experts, offs, 0), axis=-1)
    pair = jnp.argsort(slot_t.reshape(-1)).astype(I32)
    spare = n_slots + jnp.arange(MOE_PAD_TILES * tm, dtype=I32).reshape(MOE_PAD_TILES, tm)
    first = jnp.arange(MOE_PAD_TILES * tm, dtype=I32).reshape(MOE_PAD_TILES, tm) % tm
    src = jnp.concatenate([first, (pair % n).reshape(-1, tm), first], axis=0)
    dst = jnp.concatenate([spare, pair.reshape(-1, tm), spare], axis=0)
    tab = jnp.concatenate([src, dst], axis=1).reshape(-1)
    yk = _moe_experts(l, work, tab, x1t, w_gu, b_gu, w_dn, b_dn)
    return _ffn_ln(x1, yk, gate, ln_g[l][None], ln_b[l][None], split)


def _cv_in_kernel(x_ref, w_ref, b_ref, g_ref):
    xb = x_ref[...].astype(BF16)
    a = _dot(xb, w_ref[:, :CONV_W]) + b_ref[:, :CONV_W]
    gate = _dot(xb, w_ref[:, CONV_W:]) + b_ref[:, CONV_W:]
    _store_token_tiles(g_ref, a * jax.nn.sigmoid(gate))


def _cv_in_proj(x, w, b):
    n = x.shape[0]
    tm = TOKEN_TILE
    return pl.pallas_call(
        _cv_in_kernel,
        grid=(n // tm,),
        in_specs=[pl.BlockSpec((tm, D_MODEL), lambda i: (i, 0)), _full(w, 1), _full(b, 1)],
        out_specs=pl.BlockSpec((tm * TILE_ROWS, LANES), lambda i: (i, 0)),
        out_shape=jax.ShapeDtypeStruct((n * TILE_ROWS, LANES), F32),
        compiler_params=_cparams(("parallel",)),
        name="cv_in_proj",
    )(x, w, b)


CONV_GROUP = 16


def _dwconv_kernel(g_ref, past_ref, w_ref, wb_ref, lg_ref, lb_ref, *rest, bb, tt, n_prev):
    c_ref, ext, conv = rest[n_prev:]
    j = pl.program_id(1)
    halo = CONV_HALO * TILE_ROWS

    @pl.when(j == 0)
    def _():
        for b in range(bb):
            history = past_ref[b]
            for s in range(TILE_ROWS):
                rows = pl.ds((CONV_HALO - CONV_STATE) * TILE_ROWS + s, CONV_STATE, stride=TILE_ROWS)
                ext.at[b][rows, :] = history[:, s * LANES:(s + 1) * LANES]

    ext[:, halo:, :] = g_ref[...].reshape(bb, tt * TILE_ROWS, LANES)
    group = min(CONV_GROUP, tt)
    first_tap = CONV_HALO - CONV_STATE

    for b in range(bb):
        def per_group(gi, carry):
            t0 = gi * group
            acc = None
            for d in range(CONV_K):
                rows = pl.ds(pl.multiple_of((first_tap + d + t0) * TILE_ROWS, TILE_ROWS), group * TILE_ROWS)
                window = ext[b, rows, :].reshape(group, TILE_ROWS, LANES)
                term = window * w_ref[d * TILE_ROWS:(d + 1) * TILE_ROWS, :][None]
                acc = term if acc is None else acc + term
            out_rows = pl.ds(pl.multiple_of((b * tt + t0) * TILE_ROWS, TILE_ROWS), group * TILE_ROWS)
            conv[out_rows, :] = acc.reshape(group * TILE_ROWS, LANES)
            return carry

        lax.fori_loop(0, tt // group, per_group, 0)

    c = _layer_norm(_load_token_tiles(conv, bb * tt) + wb_ref[...], lg_ref[...], lb_ref[...])
    c_ref[...] = c * jax.nn.sigmoid(c)
    ext[:, 0:halo, :] = ext[:, tt * TILE_ROWS:tt * TILE_ROWS + halo, :]


def _dwconv(g, past, dw_w, dw_b, ln_g, ln_b, *, row0, b, t, bb, tt, prev):
    vec = pl.BlockSpec((1, CONV_W), lambda i, j: (0, 0))
    return _seq_call(
        functools.partial(_dwconv_kernel, bb=bb, tt=tt), name="dwconv_ln_swish",
        row0=row0, b=b, t=t, bb=bb, tt=tt, n_rows=g.shape[0] // TILE_ROWS,
        seq_ins=[g], seq_widths=[None],
        other_ins=[past, dw_w, dw_b, ln_g, ln_b],
        other_specs=[pl.BlockSpec((bb, CONV_STATE, CONV_W), lambda i, j: (i, 0, 0)), _full(dw_w, 2),
                     vec, vec, vec],
        out_widths=[CONV_W], extra_out_shapes=[], extra_out_specs=[],
        scratch=[pltpu.VMEM((bb, (CONV_HALO + tt) * TILE_ROWS, LANES), F32),
                 pltpu.VMEM((bb * tt * TILE_ROWS, LANES), F32)],
        prev_outs=prev)[0]


def _route_args(w_router, b_router, l):
    w_r = jnp.pad(w_router[l], ((0, 0), (0, LANES - N_EXPERTS))).astype(BF16)
    b_r = jnp.pad(b_router[l], (0, LANES - N_EXPERTS), constant_values=NEG_BIG)[None]
    return w_r, b_r


def kernel(x_prompt, x_sample, state_pool, state_mlstm_C, state_mlstm_n, state_mlstm_m, state_conv, w_in_pm, b_if, w_pool, pool_scale, hn_g, w_out_pm, b_out_pm, w_in_cv, b_in_cv, dw_w, dw_b, cln_g, cln_b, w_out_cv, b_out_cv, ln_mix_g, ln_mix_b, ln_ffn_g, ln_ffn_b, w_router, b_router, w_gu, b_gu, w_dn, b_dn):
    bp, tp, _ = x_prompt.shape
    bs, ts, _ = x_sample.shape
    n_p = bp * tp
    n = n_p + bs * ts
    x_srcs = [x_prompt, x_sample.reshape(bs * ts, D_MODEL)]
    chunk = TOKEN_TILE // bp
    assert chunk == MLSTM_CHUNK and tp % chunk == 0 and chunk >= max(POOL_STATE, CONV_STATE)

    groups = (
        dict(row0=0, b=bp, t=tp, pos0=0, pool_bb=bp, pool_tt=chunk, l=chunk, mlstm_bb=bp, mlstm_nc=1, conv_bb=bp,
             conv_tt=chunk, chunk_major=True,
             past_pool=jnp.zeros((bp, POOL_STATE, POOL_W), F32),
             c0=jnp.zeros((bp, MLSTM_HEADS, MLSTM_DH, MLSTM_DH), F32),
             n0=jnp.zeros((bp, MLSTM_HEADS, MLSTM_DH), F32),
             m0=jnp.zeros((bp, MLSTM_HEADS), F32),
             past_conv=jnp.zeros((bp, CONV_STATE, CONV_W), F32)),
        dict(row0=n_p, b=bs, t=ts, pos0=PAST_LEN, pool_bb=16, pool_tt=ts, l=ts, mlstm_bb=8, mlstm_nc=1, conv_bb=8,
             conv_tt=ts, chunk_major=False,
             past_pool=state_pool[0], c0=state_mlstm_C[0], n0=state_mlstm_n[0], m0=state_mlstm_m[0],
             past_conv=state_conv[0]),
    )

    def tail_state(past, a, gr, keep, tiled=False):
        b, t, row0 = gr["b"], gr["t"], gr["row0"]
        rpt = TILE_ROWS if tiled else 1
        width = a.shape[-1] * rpt
        if gr["chunk_major"]:
            ends = [row0 + (t - chunk) * b + (i + 1) * chunk for i in range(b)]
            parts = [a[(e - keep) * rpt:e * rpt] for e in ends]
            return jnp.stack(parts).reshape(b, keep, width)
        new = a[row0 * rpt:(row0 + b * t) * rpt].reshape(b, t, width)
        return jnp.concatenate([past[:, t:], new], axis=1)

    w_in = w_in_pm[0]
    w_main = w_in[:, :POOL_W + 4 * MLSTM_W].astype(BF16)
    n_gate = 2 * MLSTM_HEADS
    w_gate = jnp.pad(w_in[:, POOL_W + 4 * MLSTM_W:], ((0, 0), (0, LANES - n_gate))).astype(BF16)
    b_gate = jnp.pad(b_if[0], (0, LANES - n_gate))[None]
    w_out = w_out_pm[0].astype(BF16)
    w_r, b_r = _route_args(w_router, b_router, 0)

    u, q, k, v, o, g = _pm_in_proj(x_srcs, w_main, w_gate, b_gate)
    y_m, states = o, []
    for gr in groups:
        b, t = gr["b"], gr["t"]
        n0 = gr["n0"][:, :, None, :]
        m0 = jnp.broadcast_to(gr["m0"][:, :, None, None], (b, MLSTM_HEADS, 1, LANES))
        y_m, c_new, n_new, m_new = _mlstm(q, k, v, y_m, g, gr["c0"], n0, m0, hn_g[0][None], row0=gr["row0"], b=b,
                                          t=t, bb=gr["mlstm_bb"], l=gr["l"], nc=gr["mlstm_nc"])
        states.append((tail_state(gr["past_pool"], u, gr, POOL_STATE), c_new, n_new[:, :, 0, :], m_new[:, :, 0, 0]))
    y_pool = q
    for gr in groups:
        past = jnp.pad(gr["past_pool"], ((0, 0), (POOL_HALO - POOL_STATE, 0), (0, 0)))
        y_pool = _pool_mix(u, past, w_pool[0].astype(BF16), pool_scale[0][None], row0=gr["row0"], b=gr["b"],
                           t=gr["t"], bb=gr["pool_bb"], tt=gr["pool_tt"], pos0=gr["pos0"], prev=[y_pool])
    x1, x1t, route_t, gate, counts = _mix_out([y_pool, y_m], [w_out[:POOL_W], w_out[POOL_W:]], x_srcs,
                                              b_out_pm[0][None], ln_mix_g[0][None], ln_mix_b[0][None], w_r, b_r)
    x2 = _moe_block(0, x1, x1t, route_t, gate, counts[0, :N_EXPERTS].astype(I32), ln_ffn_g, ln_ffn_b,
                    w_gu, b_gu, w_dn, b_dn)

    w_r, b_r = _route_args(w_router, b_router, 1)
    dw_w_p = jnp.pad(dw_w[0], ((0, CONV_HALO - CONV_K), (0, 0))).reshape(CONV_HALO * TILE_ROWS, LANES)
    gl = _cv_in_proj(x2, w_in_cv[0].astype(BF16), b_in_cv[0][None])
    c, conv_states = x1, []
    for gr in groups:
        c = _dwconv(gl, gr["past_conv"], dw_w_p, dw_b[0][None], cln_g[0][None], cln_b[0][None], row0=gr["row0"], b=gr["b"],
                    t=gr["t"], bb=gr["conv_bb"], tt=gr["conv_tt"], prev=[c])
        conv_states.append(tail_state(gr["past_conv"], gl, gr, CONV_STATE, tiled=True))
    x3, x3t, route_t, gate, counts = _mix_out([c], [w_out_cv[0].astype(BF16)], [x2], b_out_cv[0][None],
                                              ln_mix_g[1][None], ln_mix_b[1][None], w_r, b_r)
    y_p, y_s = _moe_block(1, x3, x3t, route_t, gate, counts[0, :N_EXPERTS].astype(I32), ln_ffn_g, ln_ffn_b,
                          w_gu, b_gu, w_dn, b_dn, split=(bp, tp))

    (pool_p, c_p, n_p_, m_p), (pool_s, c_s, n_s, m_s) = states
    return (y_p, y_s.reshape(bs, ts, D_MODEL),
            pool_p[None], pool_s[None], c_p[None], c_s[None], n_p_[None], n_s[None], m_p[None], m_s[None],
            conv_states[0][None], conv_states[1][None])
```
